```python
import math
import jax
import jax.numpy as jnp
from jax import lax
import numpy as np

D_MODEL = 4096
BATCH = 1
SEQ = 16384
DEPTH = 4

HEAD_DIM = 128
ROPE_THETA = 500000.0
ROT_DIM = HEAD_DIM // 4
Q_BLOCK = 128
NEG_INF = -1e30
SEL_FORCE = 1e9

A_HEADS = 20
A_KV_GROUPS = 4
A_HPG = A_HEADS // A_KV_GROUPS
CMP_LEN = 32
CMP_STRIDE = 16
CMP_HIDDEN = 4 * HEAD_DIM
SLC_LEN = 64
SLC_TOPK = 16
WIN_LEN = 512

B_DILATIONS = ((128, 1), (512, 4), (2048, 16))
B_HPG = 4
B_HEADS = B_HPG * len(B_DILATIONS)

C_HEADS = 16
C_VDIM = 2 * HEAD_DIM

N_EXPERTS = 32
TOP_K = 4
D_EXPERT = 384
SWIGLU_LIMIT = 7.0
SWIGLU_ALPHA = 1.702

DEEPNORM_ALPHA = (2.0 * DEPTH) ** 0.25
DEEPNORM_BETA = (8.0 * DEPTH) ** -0.25
N_EVEN = (DEPTH + 1) // 2
N_ODD = DEPTH // 2

A_Q = A_HEADS * HEAD_DIM
A_KV = A_KV_GROUPS * HEAD_DIM
A_GATE = A_HEADS * 3
B_W = B_HEADS * HEAD_DIM
EVEN_IN = A_Q + 6 * A_KV + A_GATE + 3 * B_W
EVEN_MIX = A_Q + B_W
C_QK = 2 * C_HEADS * HEAD_DIM
C_V = C_HEADS * C_VDIM
ODD_IN = 2 * C_QK + C_V
ODD_MIX = C_V

kernel_name = 'hybrid_nsa_dilated_diffattn_moe'


def split_cols(h, sizes):
    offs = np.cumsum(sizes)[:-1].tolist()
    return jnp.split(h, offs, axis=-1)


def layer_norm(x, g, b, eps=1e-5):
    xf = x.astype(jnp.float32)
    mu = jnp.mean(xf, axis=-1, keepdims=True)
    var = jnp.mean(jnp.square(xf - mu), axis=-1, keepdims=True)
    return ((xf - mu) * lax.rsqrt(var + eps) * g + b).astype(x.dtype)


def rope(x, pos):
    half = ROT_DIM // 2
    inv = ROPE_THETA ** (-jnp.arange(half, dtype=jnp.float32) / half)
    ang = pos.astype(jnp.float32)[:, None] * inv[None, :]
    cos = jnp.cos(ang)[None, :, None, :]
    sin = jnp.sin(ang)[None, :, None, :]
    xf = x.astype(jnp.float32)
    x1 = xf[..., :half]
    x2 = xf[..., half:ROT_DIM]
    out = jnp.concatenate([x1 * cos - x2 * sin, x2 * cos + x1 * sin, xf[..., ROT_DIM:]], axis=-1)
    return out.astype(x.dtype)


def masked_probs(s, mask):
    s = jnp.where(mask, s.astype(jnp.float32), NEG_INF)
    return jax.nn.softmax(s, axis=-1) * mask


def probs_lse(s, mask):
    s = jnp.where(mask, s.astype(jnp.float32), NEG_INF)
    m = jnp.max(s, axis=-1, keepdims=True)
    e = jnp.exp(s - m)
    den = jnp.sum(e, axis=-1, keepdims=True)
    return e / den, (m + jnp.log(den))[..., 0]


def compress_blocks(kv, pe, w1, b1, w2, b2):
    bsz, t, g, dh = kv.shape
    chunks = kv.reshape(bsz, t // CMP_STRIDE, CMP_STRIDE, g, dh)
    blocks = jnp.concatenate([chunks[:, :-1], chunks[:, 1:]], axis=2) + pe[:, None, :]
    n = blocks.shape[1]
    flat = blocks.transpose(0, 1, 3, 2, 4).reshape(bsz, n, g, CMP_LEN * dh)
    hid = jax.nn.gelu(flat @ w1 + b1)
    return hid @ w2 + b2


def nsa_attention(q_a, k_c, v_c, k_s, v_s, k_w, v_w, gate, cmpk, cmpv, pos):
    bsz, t, _ = q_a.shape
    G, HPG, dh = A_KV_GROUPS, A_HPG, HEAD_DIM
    q = rope(q_a.reshape(bsz, t, A_HEADS, dh), pos) * (dh ** -0.5)
    q = q.reshape(bsz, t, G, HPG, dh).transpose(0, 2, 3, 1, 4)
    gate = jax.nn.sigmoid(gate.reshape(bsz, t, G, HPG, 3)).transpose(0, 2, 3, 1, 4)

    def grp(a):
        return a.reshape(bsz, t, G, dh)

    n_cmp = t // CMP_STRIDE - 1
    cmp_end = jnp.arange(n_cmp, dtype=jnp.int32) * CMP_STRIDE + (CMP_LEN - 1)
    kc = rope(compress_blocks(grp(k_c), *cmpk), cmp_end).transpose(0, 2, 1, 3)
    vc = compress_blocks(grp(v_c), *cmpv).transpose(0, 2, 1, 3)

    n_slc = t // SLC_LEN
    n_sel = min(SLC_TOPK, n_slc)

    def to_blocks(a):
        return a.reshape(bsz, n_slc, SLC_LEN, G, dh).transpose(0, 3, 1, 2, 4)

    ks = to_blocks(rope(grp(k_s), pos))
    vs = to_blocks(grp(v_s))
    cmp_start = cmp_end - (CMP_LEN - 1)
    slc_start = jnp.arange(n_slc, dtype=jnp.int32) * SLC_LEN
    overlap = ((cmp_start[:, None] < slc_start[None, :] + SLC_LEN)
               & (cmp_start[:, None] + CMP_LEN > slc_start[None, :])).astype(jnp.float32)
    gather = jax.vmap(jax.vmap(lambda blk, ix: blk[ix]))

    def pad_front(a):
        return jnp.pad(a.transpose(0, 2, 1, 3), ((0, 0), (0, 0), (WIN_LEN, 0), (0, 0)))

    kw = pad_front(rope(grp(k_w), pos))
    vw = pad_front(grp(v_w))

    def block(s0):
        tq = s0 + jnp.arange(Q_BLOCK, dtype=jnp.int32)
        qb = lax.dynamic_slice_in_dim(q, s0, Q_BLOCK, axis=3)
        gb = lax.dynamic_slice_in_dim(gate, s0, Q_BLOCK, axis=3)
        s_c = jnp.einsum('bghqd,bgkd->bghqk', qb, kc)
        p_c = masked_probs(s_c, cmp_end[None, :] <= tq[:, None])
        o_c = jnp.einsum('bghqk,bgkd->bghqd', p_c.astype(vc.dtype), vc)
        imp = jnp.einsum('bghqk,kj->bgqj', p_c, overlap)
        j = jnp.arange(n_slc, dtype=jnp.int32)[None, :]
        tb = (tq // SLC_LEN)[:, None]
        forced = (j == 0) | (j == tb) | (j == tb - 1)
        imp = jnp.where(forced, SEL_FORCE, jnp.where(j <= tb, imp, -SEL_FORCE))
        top_s, idx = lax.top_k(imp, n_sel)
        valid = top_s > -0.5 * SEL_FORCE
        ksel = gather(ks, idx).reshape(bsz, G, Q_BLOCK, n_sel * SLC_LEN, dh)
        vsel = gather(vs, idx).reshape(bsz, G, Q_BLOCK, n_sel * SLC_LEN, dh)
        kpos = idx[..., None] * SLC_LEN + jnp.arange(SLC_LEN, dtype=jnp.int32)
        m_s = (valid[..., None] & (kpos <= tq[None, None, :, None, None])).reshape(
            bsz, G, Q_BLOCK, n_sel * SLC_LEN)
        s_s = jnp.einsum('bghqd,bgqkd->bghqk', qb, ksel)
        p_s = masked_probs(s_s, m_s[:, :, None])
        o_s = jnp.einsum('bghqk,bgqkd->bghqd', p_s.astype(vsel.dtype), vsel)
        kwb = lax.dynamic_slice_in_dim(kw, s0, WIN_LEN + Q_BLOCK, axis=2)
        vwb = lax.dynamic_slice_in_dim(vw, s0, WIN_LEN + Q_BLOCK, axis=2)
        kp = s0 - WIN_LEN + jnp.arange(WIN_LEN + Q_BLOCK, dtype=jnp.int32)
        dist = tq[:, None] - kp[None, :]
        m_w = (dist >= 0) & (dist < WIN_LEN) & (kp[None, :] >= 0)
        s_w = jnp.einsum('bghqd,bgkd->bghqk', qb, kwb)
        p_w = masked_probs(s_w, m_w)
        o_w = jnp.einsum('bghqk,bgkd->bghqd', p_w.astype(vwb.dtype), vwb)
        return gb[..., 0:1] * o_c + gb[..., 1:2] * o_s + gb[..., 2:3] * o_w

    starts = jnp.arange(t // Q_BLOCK, dtype=jnp.int32) * Q_BLOCK
    out = lax.map(block, starts)
    return out.transpose(1, 0, 4, 2, 3, 5).reshape(bsz, t, A_HEADS * dh)


def dilated_attention(q_b, k_b, v_b, pos):
    bsz, t, _ = q_b.shape
    dh = HEAD_DIM

    def heads(a):
        return a.reshape(bsz, t, B_HEADS, dh)

    q = (rope(heads(q_b), pos) * (dh ** -0.5)).transpose(0, 2, 1, 3)
    k = rope(heads(k_b), pos).transpose(0, 2, 1, 3)
    v = heads(v_b).transpose(0, 2, 1, 3)

    def block(s0):
        tq = s0 + jnp.arange(Q_BLOCK, dtype=jnp.int32)
        outs, lses = [], []
        for g, (window, dilation) in enumerate(B_DILATIONS):
            h0, h1 = g * B_HPG, (g + 1) * B_HPG
            n_k = window // dilation + 1
            kidx = tq[:, None] - dilation * jnp.arange(n_k, dtype=jnp.int32)[None, :]
            valid = kidx >= 0
            kidx = jnp.maximum(kidx, 0)
            qg = lax.dynamic_slice_in_dim(q[:, h0:h1], s0, Q_BLOCK, axis=2)
            kg = k[:, h0:h1, kidx]
            vg = v[:, h0:h1, kidx]
            s = jnp.einsum('bhqd,bhqkd->bhqk', qg, kg)
            p, lse = probs_lse(s, valid)
            outs.append(jnp.einsum('bhqk,bhqkd->bhqd', p.astype(vg.dtype), vg))
            lses.append(lse)
        w = jax.nn.softmax(jnp.stack(lses, axis=0), axis=0)
        o = jnp.stack(outs, axis=0) * w[..., None].astype(outs[0].dtype)
        return o.transpose(1, 0, 2, 3, 4).reshape(bsz, B_HEADS, Q_BLOCK, dh)

    starts = jnp.arange(t // Q_BLOCK, dtype=jnp.int32) * Q_BLOCK
    out = lax.map(block, starts)
    return out.transpose(1, 0, 3, 2, 4).reshape(bsz, t, B_W)


def even_mixer(x, w_in, w_out, cmpk, cmpv):
    t = x.shape[1]
    pos = jnp.arange(t, dtype=jnp.int32)
    q_a, k_c, v_c, k_s, v_s, k_w, v_w, gate, q_b, k_b, v_b = split_cols(
        x @ w_in, (A_Q, A_KV, A_KV, A_KV, A_KV, A_KV, A_KV, A_GATE, B_W, B_W, B_W))
    o_a = nsa_attention(q_a, k_c, v_c, k_s, v_s, k_w, v_w, gate, cmpk, cmpv, pos)
    o_b = dilated_attention(q_b, k_b, v_b, pos)
    return jnp.concatenate([o_a, o_b], axis=-1) @ w_out


def odd_mixer(x, w_in, w_out, lq1, lk1, lq2, lk2, sub_g, layer):
    bsz, t, _ = x.shape
    dh = HEAD_DIM
    pos = jnp.arange(t, dtype=jnp.int32)
    q, k, v = split_cols(x @ w_in, (C_QK, C_QK, C_V))
    q = (rope(q.reshape(bsz, t, 2 * C_HEADS, dh), pos) * (dh ** -0.5)).transpose(0, 2, 1, 3)
    k = rope(k.reshape(bsz, t, 2 * C_HEADS, dh), pos).transpose(0, 2, 1, 3)
    v = v.reshape(bsz, t, C_HEADS, C_VDIM).transpose(0, 2, 1, 3)
    lam_init = 0.8 - 0.6 * math.exp(-0.3 * layer)
    f32 = jnp.float32
    lam = (jnp.exp(jnp.sum(lq1.astype(f32) * lk1.astype(f32)))
           - jnp.exp(jnp.sum(lq2.astype(f32) * lk2.astype(f32))) + lam_init)

    def block(s0):
        tq = s0 + jnp.arange(Q_BLOCK, dtype=jnp.int32)
        qb = lax.dynamic_slice_in_dim(q, s0, Q_BLOCK, axis=2)
        s = jnp.einsum('bhqd,bhkd->bhqk', qb, k)
        p = masked_probs(s, pos[None, :] <= tq[:, None]).reshape(bsz, C_HEADS, 2, Q_BLOCK, t)
        a = p[:, :, 0] - lam * p[:, :, 1]
        return jnp.einsum('bhqk,bhkd->bhqd', a.astype(v.dtype), v)

    starts = jnp.arange(t // Q_BLOCK, dtype=jnp.int32) * Q_BLOCK
    o = lax.map(block, starts).transpose(1, 0, 3, 2, 4).reshape(bsz, t, C_HEADS, C_VDIM)
    of = o.astype(f32)
    of = of * lax.rsqrt(jnp.mean(jnp.square(of), axis=-1, keepdims=True) + 1e-5) * sub_g * (1.0 - lam_init)
    return of.astype(x.dtype).reshape(bsz, t, ODD_MIX) @ w_out


def moe_ffn(x, rw, rb, wgu, bgu, wd, bd):
    bsz, t, d = x.shape
    xt = x.reshape(bsz * t, d)
    logits = (xt @ rw + rb).astype(jnp.float32)
    top_l, top_i = lax.top_k(logits, TOP_K)
    top_w = jax.nn.softmax(top_l, axis=-1)
    comb = jnp.einsum('nk,nke->ne', top_w,
                      jax.nn.one_hot(top_i, N_EXPERTS, dtype=jnp.float32)).astype(x.dtype)
    h = jnp.einsum('nd,edf->nef', xt, wgu) + bgu
    glu = jnp.minimum(h[..., ::2], SWIGLU_LIMIT)
    lin = jnp.clip(h[..., 1::2], -SWIGLU_LIMIT, SWIGLU_LIMIT)
    act = glu * jax.nn.sigmoid(SWIGLU_ALPHA * glu) * (lin + 1.0) * comb[..., None]
    y = jnp.einsum('nef,efd->nd', act, wd) + comb @ bd
    return y.reshape(bsz, t, d)


def setup_inputs(seed: int = 0) -> dict:
    key = jax.random.key(seed)
    keys = iter(jax.random.split(key, 32))

    def nrm(shape, scale):
        return jax.random.normal(next(keys), shape, jnp.float32) * scale

    d = D_MODEL
    fl = CMP_LEN * HEAD_DIM
    return {
        'x': nrm((BATCH, SEQ, d), 1.0),
        'even_w_in': nrm((N_EVEN, d, EVEN_IN), d ** -0.5),
        'even_w_out': nrm((N_EVEN, EVEN_MIX, d), DEEPNORM_BETA * EVEN_MIX ** -0.5),
        'cmpk_pe': nrm((N_EVEN, CMP_LEN, HEAD_DIM), 0.02),
        'cmpk_w1': nrm((N_EVEN, fl, CMP_HIDDEN), fl ** -0.5),
        'cmpk_b1': nrm((N_EVEN, CMP_HIDDEN), 0.02),
        'cmpk_w2': nrm((N_EVEN, CMP_HIDDEN, HEAD_DIM), CMP_HIDDEN ** -0.5),
        'cmpk_b2': nrm((N_EVEN, HEAD_DIM), 0.02),
        'cmpv_pe': nrm((N_EVEN, CMP_LEN, HEAD_DIM), 0.02),
        'cmpv_w1': nrm((N_EVEN, fl, CMP_HIDDEN), fl ** -0.5),
        'cmpv_b1': nrm((N_EVEN, CMP_HIDDEN), 0.02),
        'cmpv_w2': nrm((N_EVEN, CMP_HIDDEN, HEAD_DIM), CMP_HIDDEN ** -0.5),
        'cmpv_b2': nrm((N_EVEN, HEAD_DIM), 0.02),
        'odd_w_in': nrm((N_ODD, d, ODD_IN), d ** -0.5),
        'odd_w_out': nrm((N_ODD, ODD_MIX, d), DEEPNORM_BETA * ODD_MIX ** -0.5),
        'lam_q1': nrm((N_ODD, HEAD_DIM), 0.1),
        'lam_k1': nrm((N_ODD, HEAD_DIM), 0.1),
        'lam_q2': nrm((N_ODD, HEAD_DIM), 0.1),
        'lam_k2': nrm((N_ODD, HEAD_DIM), 0.1),
        'subln_g': 1.0 + nrm((N_ODD, C_VDIM), 0.02),
        'ln_mix_g': 1.0 + nrm((DEPTH, d), 0.02),
        'ln_mix_b': nrm((DEPTH, d), 0.02),
        'ln_ffn_g': 1.0 + nrm((DEPTH, d), 0.02),
        'ln_ffn_b': nrm((DEPTH, d), 0.02),
        'router_w': nrm((DEPTH, d, N_EXPERTS), d ** -0.5),
        'router_b': nrm((DEPTH, N_EXPERTS), 0.01),
        'exp_w_gu': nrm((DEPTH, N_EXPERTS, d, 2 * D_EXPERT), d ** -0.5),
        'exp_b_gu': nrm((DEPTH, N_EXPERTS, 2 * D_EXPERT), 0.02),
        'exp_w_down': nrm((DEPTH, N_EXPERTS, D_EXPERT, d), DEEPNORM_BETA * D_EXPERT ** -0.5),
        'exp_b_down': nrm((DEPTH, N_EXPERTS, d), 0.02),
    }


def reference(x, even_w_in, even_w_out, cmpk_pe, cmpk_w1, cmpk_b1, cmpk_w2, cmpk_b2,
              cmpv_pe, cmpv_w1, cmpv_b1, cmpv_w2, cmpv_b2, odd_w_in, odd_w_out,
              lam_q1, lam_k1, lam_q2, lam_k2, subln_g, ln_mix_g, ln_mix_b, ln_ffn_g, ln_ffn_b,
              router_w, router_b, exp_w_gu, exp_b_gu, exp_w_down, exp_b_down):
    for layer in range(DEPTH):
        if layer % 2 == 0:
            e = layer // 2
            mix = even_mixer(x, even_w_in[e], even_w_out[e],
                             (cmpk_pe[e], cmpk_w1[e], cmpk_b1[e], cmpk_w2[e], cmpk_b2[e]),
                             (cmpv_pe[e], cmpv_w1[e], cmpv_b1[e], cmpv_w2[e], cmpv_b2[e]))
        else:
            o = layer // 2
            mix = odd_mixer(x, odd_w_in[o], odd_w_out[o], lam_q1[o], lam_k1[o],
                            lam_q2[o], lam_k2[o], subln_g[o], layer)
        x = layer_norm(DEEPNORM_ALPHA * x + mix, ln_mix_g[layer], ln_mix_b[layer])
        ffn = moe_ffn(x, router_w[layer], router_b[layer], exp_w_gu[layer], exp_b_gu[layer],
                      exp_w_down[layer], exp_b_down[layer])
        x = layer_norm(DEEPNORM_ALPHA * x + ffn, ln_ffn_g[layer], ln_ffn_b[layer])
    return x
```

```python
import functools
import math

import jax
import jax.numpy as jnp
import numpy as np
from jax import lax
from jax.experimental import pallas as pl
from jax.experimental.pallas import tpu as pltpu

F32 = jnp.float32
BF16 = jnp.bfloat16

DEPTH = 4
HEAD_DIM = 128
ROPE_THETA = 500000.0
ROT_DIM = HEAD_DIM // 4
ROT_HALF = ROT_DIM // 2
NEG_INF = -1e30
SEL_FORCE = 1e9

A_HEADS = 20
A_KV_GROUPS = 4
A_HPG = A_HEADS // A_KV_GROUPS
CMP_LEN = 32
CMP_STRIDE = 16
CMP_HIDDEN = 4 * HEAD_DIM
SLC_LEN = 64
SLC_TOPK = 16
WIN_LEN = 512

B_DILATIONS = ((128, 1), (512, 4), (2048, 16))
B_HPG = 4
B_HEADS = B_HPG * len(B_DILATIONS)

C_HEADS = 16
C_VDIM = 2 * HEAD_DIM

N_EXPERTS = 32
TOP_K = 4
D_EXPERT = 384
SWIGLU_LIMIT = 7.0
SWIGLU_ALPHA = 1.702

DEEPNORM_ALPHA = (2.0 * DEPTH) ** 0.25

A_Q = A_HEADS * HEAD_DIM
A_KV = A_KV_GROUPS * HEAD_DIM
A_GATE = A_HEADS * 3
B_W = B_HEADS * HEAD_DIM
C_QK = 2 * C_HEADS * HEAD_DIM
C_V = C_HEADS * C_VDIM

LANE = 128
VMEM_LIMIT = 56 * 1024 * 1024

GATE_PAD = 512
EVEN_COLS = A_Q + 6 * A_KV + 3 * B_W + GATE_PAD
CB_QA = 0
CB_KC = A_Q // LANE
CB_VC = CB_KC + A_KV // LANE
CB_KS = CB_VC + A_KV // LANE
CB_VS = CB_KS + A_KV // LANE
CB_KW = CB_VS + A_KV // LANE
CB_VW = CB_KW + A_KV // LANE
CB_QB = CB_VW + A_KV // LANE
CB_KB = CB_QB + B_W // LANE
CB_VB = CB_KB + B_W // LANE
CB_GATE = CB_VB + B_W // LANE

MM_TN = 512


def _cparams(sem):
    return pltpu.CompilerParams(dimension_semantics=sem, vmem_limit_bytes=VMEM_LIMIT)


def _nt_dot(a, b):
    return lax.dot_general(a, b, (((1,), (1,)), ((), ())), preferred_element_type=F32)


def _rope_lanes(x, tab):
    c = tab[:, 0:LANE]
    sa = tab[:, LANE:2 * LANE]
    sb = tab[:, 2 * LANE:3 * LANE]
    return (x * c + pltpu.roll(x, LANE - ROT_HALF, axis=1) * sa
            + pltpu.roll(x, ROT_HALF, axis=1) * sb)


def _mm_kernel(*refs, has_bias, act, has_rope, n_sub):
    if has_rope:
        modes_ref, a_ref, b_ref = refs[0], refs[1], refs[2]
        rest = refs[3:]
    else:
        a_ref, b_ref = refs[0], refs[1]
        rest = refs[2:]
    idx = 0
    if has_bias:
        bias_ref = rest[idx]
        idx += 1
    if has_rope:
        tab_ref = rest[idx]
        idx += 1
    o_ref = rest[idx]

    acc = jnp.dot(a_ref[...], b_ref[...], preferred_element_type=F32)
    if has_bias:
        acc = acc + bias_ref[...]
    if act == "gelu":
        acc = jax.nn.gelu(acc, approximate=True)
    if not has_rope:
        o_ref[...] = acc.astype(o_ref.dtype)
        return
    mode = modes_ref[pl.program_id(1)]

    @pl.when(mode == 0)
    def _():
        o_ref[...] = acc.astype(o_ref.dtype)

    @pl.when(mode != 0)
    def _():
        tab = tab_ref[...]
        for s in range(n_sub):
            sl = slice(s * LANE, (s + 1) * LANE)
            o_ref[:, sl] = _rope_lanes(acc[:, sl], tab).astype(o_ref.dtype)


def _mm(a, b, *, bias=None, act=None, rope_tabs=None, rope_modes=None,
        out_dtype=None, tm=1024, tn=MM_TN, name="mm"):
    out_dtype = BF16 if out_dtype is None else out_dtype
    m, k = a.shape
    k2, n = b.shape
    assert k == k2
    tm = min(tm, m)
    tn = min(tn, n)
    assert m % tm == 0 and n % tn == 0
    has_bias = bias is not None
    has_rope = rope_tabs is not None
    grid = (m // tm, n // tn)
    kern = functools.partial(_mm_kernel, has_bias=has_bias, act=act, has_rope=has_rope,
                             n_sub=tn // LANE)
    if has_rope:
        in_specs = [pl.BlockSpec((tm, k), lambda i, j, md: (i, 0)),
                    pl.BlockSpec((k, tn), lambda i, j, md: (0, j))]
        args = [a, b]
        if has_bias:
            in_specs.append(pl.BlockSpec((1, tn), lambda i, j, md: (0, j)))
            args.append(bias.reshape(1, n).astype(F32))
        in_specs.append(pl.BlockSpec((None, tm, 3 * LANE),
                                     lambda i, j, md: (jnp.maximum(md[j] - 1, 0), i, 0)))
        args.append(rope_tabs)
        gs = pltpu.PrefetchScalarGridSpec(
            num_scalar_prefetch=1, grid=grid, in_specs=in_specs,
            out_specs=pl.BlockSpec((tm, tn), lambda i, j, md: (i, j)))
        return pl.pallas_call(kern, grid_spec=gs,
                              out_shape=jax.ShapeDtypeStruct((m, n), out_dtype),
                              compiler_params=_cparams(("parallel", "arbitrary")),
                              name=name)(rope_modes, *args)
    in_specs = [pl.BlockSpec((tm, k), lambda i, j: (i, 0)),
                pl.BlockSpec((k, tn), lambda i, j: (0, j))]
    args = [a, b]
    if has_bias:
        in_specs.append(pl.BlockSpec((1, tn), lambda i, j: (0, j)))
        args.append(bias.reshape(1, n).astype(F32))
    return pl.pallas_call(kern, grid=grid, in_specs=in_specs,
                          out_specs=pl.BlockSpec((tm, tn), lambda i, j: (i, j)),
                          out_shape=jax.ShapeDtypeStruct((m, n), out_dtype),
                          compiler_params=_cparams(("parallel", "arbitrary")),
                          name=name)(*args)


def _ln_kernel(x_ref, y_ref, g_ref, b_ref, of_ref, ob_ref):
    z = DEEPNORM_ALPHA * x_ref[...] + y_ref[...].astype(F32)
    mu = jnp.mean(z, axis=-1, keepdims=True)
    zc = z - mu
    var = jnp.mean(zc * zc, axis=-1, keepdims=True)
    out = zc * lax.rsqrt(var + 1e-5) * g_ref[...] + b_ref[...]
    of_ref[...] = out
    ob_ref[...] = out.astype(BF16)


def _ln_res(x, y, g, b, tm=256):
    t, d = x.shape
    tm = min(tm, t)
    row = pl.BlockSpec((tm, d), lambda i: (i, 0))
    vec = pl.BlockSpec((1, d), lambda i: (0, 0))
    return pl.pallas_call(
        _ln_kernel, grid=(t // tm,), in_specs=[row, row, vec, vec], out_specs=[row, row],
        out_shape=[jax.ShapeDtypeStruct((t, d), F32), jax.ShapeDtypeStruct((t, d), BF16)],
        compiler_params=_cparams(("parallel",)), name="ln_res",
    )(x, y, g.reshape(1, d), b.reshape(1, d))


def _rope_tables(pos, scale):
    inv = ROPE_THETA ** (-jnp.arange(ROT_HALF, dtype=F32) / ROT_HALF)
    ang = pos.astype(F32)[:, None] * inv[None, :]
    cos = jnp.cos(ang)
    sin = jnp.sin(ang)
    n = pos.shape[0]
    ones = jnp.ones((n, HEAD_DIM - ROT_DIM), F32)
    zer = jnp.zeros((n, HEAD_DIM - ROT_HALF), F32)
    c = jnp.concatenate([cos, cos, ones], axis=1)
    sa = jnp.concatenate([-sin, zer], axis=1)
    sb = jnp.concatenate([jnp.zeros((n, ROT_HALF), F32), sin,
                          jnp.zeros((n, HEAD_DIM - ROT_DIM), F32)], axis=1)
    return jnp.concatenate([c, sa, sb], axis=1) * scale


def _swa_kernel(q_ref, kp_ref, kc_ref, vp_ref, vc_ref, *out_refs, nh, shared_kv, max_dist,
                tq, with_lse):
    o_ref = out_refs[0]
    i = pl.program_id(1)
    row = lax.broadcasted_iota(jnp.int32, (tq, 2 * tq), 0)
    col = lax.broadcasted_iota(jnp.int32, (tq, 2 * tq), 1)
    dist = row + tq - col
    mask = (dist >= 0) & (dist <= max_dist) & (col + (i - 1) * tq >= 0)
    for h in range(nh):
        kv = 0 if shared_kv else h
        hs = slice(h * LANE, (h + 1) * LANE)
        ks = slice(kv * LANE, (kv + 1) * LANE)
        q = q_ref[:, hs]
        k = jnp.concatenate([kp_ref[:, ks], kc_ref[:, ks]], axis=0)
        v = jnp.concatenate([vp_ref[:, ks], vc_ref[:, ks]], axis=0)
        s = jnp.where(mask, _nt_dot(q, k), NEG_INF)
        m = jnp.max(s, axis=-1, keepdims=True)
        e = jnp.exp(s - m)
        den = jnp.sum(e, axis=-1, keepdims=True)
        o = jnp.dot(e.astype(BF16), v, preferred_element_type=F32) / den
        o_ref[:, hs] = o.astype(o_ref.dtype)
        if with_lse:
            out_refs[1][:, hs] = jnp.broadcast_to(m + jnp.log(den), (tq, LANE))


def _swa(qsrc, ksrc, vsrc, *, n_r, n_tiles, tq, nh, shared_kv, max_dist, q_map, k_map, v_map,
         out_cols, o_map, with_lse, name):
    kvw = LANE if shared_kv else nh * LANE
    qw = nh * LANE
    prev = lambda f: (lambda r, i: (jnp.maximum(i - 1, 0), f(r)))
    cur = lambda f: (lambda r, i: (i, f(r)))
    in_specs = [pl.BlockSpec((tq, qw), cur(q_map)),
                pl.BlockSpec((tq, kvw), prev(k_map)), pl.BlockSpec((tq, kvw), cur(k_map)),
                pl.BlockSpec((tq, kvw), prev(v_map)), pl.BlockSpec((tq, kvw), cur(v_map))]
    rows = n_tiles * tq
    out_shape = [jax.ShapeDtypeStruct((rows, out_cols), BF16)]
    out_specs = [pl.BlockSpec((tq, qw), cur(o_map))]
    if with_lse:
        out_shape.append(jax.ShapeDtypeStruct((rows, out_cols), F32))
        out_specs.append(pl.BlockSpec((tq, qw), cur(o_map)))
    kern = functools.partial(_swa_kernel, nh=nh, shared_kv=shared_kv, max_dist=max_dist, tq=tq,
                             with_lse=with_lse)
    return pl.pallas_call(kern, grid=(n_r, n_tiles), in_specs=in_specs, out_specs=out_specs,
                          out_shape=out_shape,
                          compiler_params=_cparams(("parallel", "arbitrary")),
                          name=name)(qsrc, ksrc, ksrc, vsrc, vsrc)


def _cmp_kernel(q_ref, kc_ref, vc_ref, ov_ref, oc_ref, sel_ref, *, tq, n_cmp_pad, n_slc):
    i = pl.program_id(1)
    tqv = i * tq + lax.broadcasted_iota(jnp.int32, (tq, 1), 0)
    cmp_end = CMP_STRIDE * lax.broadcasted_iota(jnp.int32, (1, n_cmp_pad), 1) + (CMP_LEN - 1)
    maskc = cmp_end <= tqv
    kc = kc_ref[...]
    vc = vc_ref[...]
    psum = jnp.zeros((tq, n_cmp_pad), F32)
    for h in range(A_HPG):
        hs = slice(h * LANE, (h + 1) * LANE)
        s = jnp.where(maskc, _nt_dot(q_ref[:, hs], kc), NEG_INF)
        m = jnp.max(s, axis=-1, keepdims=True)
        e = jnp.where(maskc, jnp.exp(s - m), 0.0)
        den = jnp.sum(e, axis=-1, keepdims=True)
        p = e / jnp.maximum(den, 1e-30)
        oc_ref[:, hs] = jnp.dot(p.astype(BF16), vc, preferred_element_type=F32).astype(oc_ref.dtype)
        psum = psum + p
    ov = ov_ref[...]
    p1 = psum.astype(BF16)
    r1 = psum - p1.astype(F32)
    p2 = r1.astype(BF16)
    p3 = (r1 - p2.astype(F32)).astype(BF16)
    imp = (jnp.dot(p1, ov, preferred_element_type=F32) + jnp.dot(p2, ov, preferred_element_type=F32)
           + jnp.dot(p3, ov, preferred_element_type=F32))
    j = lax.broadcasted_iota(jnp.int32, (1, n_slc), 1)
    jf = j.astype(F32)
    tb = tqv // SLC_LEN
    forced = (j == 0) | (j == tb) | (j == tb - 1)
    val = jnp.where(forced, SEL_FORCE, jnp.where(j <= tb, imp, -SEL_FORCE))
    sel = jnp.zeros((tq, n_slc), F32)
    for _ in range(min(SLC_TOPK, n_slc)):
        m = jnp.max(val, axis=-1, keepdims=True)
        first = jnp.min(jnp.where(val == m, jf, float(n_slc)), axis=-1, keepdims=True)
        hit = jf == first
        sel = jnp.where(hit & (m > -0.5 * SEL_FORCE), 1.0, sel)
        val = jnp.where(hit, -3e38, val)
    sel_ref[...] = sel.astype(sel_ref.dtype)


def _cmp_attn(h, kc, vc, t, tq=128):
    n_cmp_pad = kc.shape[0] // A_KV_GROUPS
    n_slc = t // SLC_LEN
    ci = np.arange(n_cmp_pad)[:, None] * CMP_STRIDE
    sj = np.arange(n_slc)[None, :] * SLC_LEN
    overlap = ((ci < sj + SLC_LEN) & (ci + CMP_LEN > sj) & (np.arange(n_cmp_pad)[:, None] < t // CMP_STRIDE - 1))
    overlap = jnp.asarray(overlap.astype(np.float32), BF16)
    kern = functools.partial(_cmp_kernel, tq=tq, n_cmp_pad=n_cmp_pad, n_slc=n_slc)
    qw = A_HPG * LANE
    return pl.pallas_call(
        kern, grid=(A_KV_GROUPS, t // tq),
        in_specs=[pl.BlockSpec((tq, qw), lambda g, i: (i, g)),
                  pl.BlockSpec((n_cmp_pad, LANE), lambda g, i: (g, 0)),
                  pl.BlockSpec((n_cmp_pad, LANE), lambda g, i: (g, 0)),
                  pl.BlockSpec((n_cmp_pad, n_slc), lambda g, i: (0, 0))],
        out_specs=[pl.BlockSpec((tq, qw), lambda g, i: (i, g)),
                   pl.BlockSpec((tq, n_slc), lambda g, i: (i, g))],
        out_shape=[jax.ShapeDtypeStruct((t, A_Q), BF16),
                   jax.ShapeDtypeStruct((t, A_KV_GROUPS * n_slc), BF16)],
        compiler_params=_cparams(("parallel", "arbitrary")), name="cmp_attn_topk",
    )(h, kc, vc, overlap)


def _sel_kernel(q_ref, k_ref, v_ref, sel_ref, o_ref, *, tq, tk, n_slc):
    i = pl.program_id(1)
    nh = A_HPG
    q5 = jnp.concatenate([q_ref[:, h * LANE:(h + 1) * LANE] for h in range(nh)], axis=0)
    selb = sel_ref[...]
    tqv = i * tq + lax.broadcasted_iota(jnp.int32, (tq, 1), 0)
    blk_per_tile = tk // SLC_LEN
    ej = lax.broadcasted_iota(jnp.int32, (n_slc, tk), 0)
    ec = lax.broadcasted_iota(jnp.int32, (n_slc, tk), 1) // SLC_LEN
    kcol = lax.broadcasted_iota(jnp.int32, (1, tk), 1)
    n_kv = (i * tq + tq + tk - 1) // tk

    def body(t, carry):
        m, l, acc = carry
        start = pl.multiple_of(t * tk, tk)
        k = k_ref[pl.ds(start, tk), :]
        v = v_ref[pl.ds(start, tk), :]
        expand = jnp.where(ej == ec + t * blk_per_tile, 1.0, 0.0).astype(BF16)
        picked = jnp.dot(selb, expand, preferred_element_type=F32)
        valid = (picked > 0.5) & (kcol + t * tk <= tqv)
        valid5 = jnp.concatenate([valid] * nh, axis=0)
        s = jnp.where(valid5, _nt_dot(q5, k), NEG_INF)
        m_new = jnp.maximum(m, jnp.max(s, axis=-1, keepdims=True))
        alpha = jnp.exp(m - m_new)
        p = jnp.exp(s - m_new)
        l = alpha * l + jnp.sum(p, axis=-1, keepdims=True)
        acc = alpha * acc + jnp.dot(p.astype(BF16), v, preferred_element_type=F32)
        return m_new, l, acc

    init = (jnp.full((nh * tq, 1), NEG_INF, F32), jnp.zeros((nh * tq, 1), F32),
            jnp.zeros((nh * tq, LANE), F32))
    m, l, acc = lax.fori_loop(0, n_kv, body, init)
    o = acc / l
    for h in range(nh):
        o_ref[:, h * LANE:(h + 1) * LANE] = o[h * tq:(h + 1) * tq].astype(o_ref.dtype)


def _sel_attn(h, sel, t, tq=128, tk=512):
    n_slc = t // SLC_LEN
    tk = min(tk, t)
    qw = A_HPG * LANE
    kern = functools.partial(_sel_kernel, tq=tq, tk=tk, n_slc=n_slc)
    return pl.pallas_call(
        kern, grid=(A_KV_GROUPS, t // tq),
        in_specs=[pl.BlockSpec((tq, qw), lambda g, i: (i, g)),
                  pl.BlockSpec((t, LANE), lambda g, i: (0, CB_KS + g)),
                  pl.BlockSpec((t, LANE), lambda g, i: (0, CB_VS + g)),
                  pl.BlockSpec((tq, n_slc), lambda g, i: (i, g))],
        out_specs=pl.BlockSpec((tq, qw), lambda g, i: (i, g)),
        out_shape=jax.ShapeDtypeStruct((t, A_Q), BF16),
        compiler_params=_cparams(("parallel", "arbitrary")), name="sel_attn",
    )(h, h, h, sel)


def _mixprep_kernel(oc_ref, os_ref, ow_ref, gate_ref, ob_ref, lse_ref, out_ref):
    gate = jax.nn.sigmoid(gate_ref[...].astype(F32))
    for h in range(A_HEADS):
        hs = slice(h * LANE, (h + 1) * LANE)
        o = (gate[:, 3 * h:3 * h + 1] * oc_ref[:, hs].astype(F32)
             + gate[:, 3 * h + 1:3 * h + 2] * os_ref[:, hs].astype(F32)
             + gate[:, 3 * h + 2:3 * h + 3] * ow_ref[:, hs].astype(F32))
        out_ref[:, hs] = o.astype(out_ref.dtype)
    ng = len(B_DILATIONS)
    for hi in range(B_HPG):
        lses = [lse_ref[:, (g * B_HPG + hi) * LANE:(g * B_HPG + hi + 1) * LANE] for g in range(ng)]
        m = functools.reduce(jnp.maximum, lses)
        es = [jnp.exp(x - m) for x in lses]
        den = functools.reduce(lambda a, b: a + b, es)
        for g in range(ng):
            src = slice((g * B_HPG + hi) * LANE, (g * B_HPG + hi + 1) * LANE)
            dst = slice(A_Q + (g * B_HPG + hi) * LANE, A_Q + (g * B_HPG + hi + 1) * LANE)
            out_ref[:, dst] = (ob_ref[:, src].astype(F32) * (es[g] / den)).astype(out_ref.dtype)


def _mixprep(o_c, o_s, o_w, h, o_b, lse_b, t, tq=256):
    tq = min(tq, t)
    a = pl.BlockSpec((tq, A_Q), lambda i: (i, 0))
    b = pl.BlockSpec((tq, B_W), lambda i: (i, 0))
    return pl.pallas_call(
        _mixprep_kernel, grid=(t // tq,),
        in_specs=[a, a, a, pl.BlockSpec((tq, LANE), lambda i: (i, CB_GATE)), b, b],
        out_specs=pl.BlockSpec((tq, A_Q + B_W), lambda i: (i, 0)),
        out_shape=jax.ShapeDtypeStruct((t, A_Q + B_W), BF16),
        compiler_params=_cparams(("parallel",)), name="even_mixprep",
    )(o_c, o_s, o_w, h, o_b, lse_b)


def _diff_kernel(q_ref, k_ref, v_ref, lq1_ref, lk1_ref, lq2_ref, lk2_ref, g_ref, o_ref, *,
                 tq, lam_init):
    i = pl.program_id(1)
    q1 = q_ref[:, 0:LANE]
    q2 = q_ref[:, LANE:2 * LANE]
    lam = (jnp.exp(jnp.sum(lq1_ref[...] * lk1_ref[...], axis=-1, keepdims=True))
           - jnp.exp(jnp.sum(lq2_ref[...] * lk2_ref[...], axis=-1, keepdims=True)) + lam_init)

    def tile(t, carry, masked):
        m1, l1, a1, m2, l2, a2 = carry
        start = pl.multiple_of(t * tq, tq)
        k = k_ref[pl.ds(start, tq), :]
        v = v_ref[pl.ds(start, tq), :]
        s1 = _nt_dot(q1, k[:, 0:LANE])
        s2 = _nt_dot(q2, k[:, LANE:2 * LANE])
        if masked:
            row = lax.broadcasted_iota(jnp.int32, (tq, tq), 0)
            col = lax.broadcasted_iota(jnp.int32, (tq, tq), 1)
            s1 = jnp.where(col <= row, s1, NEG_INF)
            s2 = jnp.where(col <= row, s2, NEG_INF)
        out = []
        for s, m, l, a in ((s1, m1, l1, a1), (s2, m2, l2, a2)):
            m_new = jnp.maximum(m, jnp.max(s, axis=-1, keepdims=True))
            alpha = jnp.exp(m - m_new)
            p = jnp.exp(s - m_new)
            l = alpha * l + jnp.sum(p, axis=-1, keepdims=True)
            a = alpha * a + jnp.dot(p.astype(BF16), v, preferred_element_type=F32)
            out += [m_new, l, a]
        return tuple(out)

    init = (jnp.full((tq, 1), NEG_INF, F32), jnp.zeros((tq, 1), F32), jnp.zeros((tq, C_VDIM), F32)) * 2
    carry = lax.fori_loop(0, i, lambda t, c: tile(t, c, False), init)
    m1, l1, a1, m2, l2, a2 = tile(i, carry, True)
    o = a1 / l1 - lam * (a2 / l2)
    o = o * lax.rsqrt(jnp.mean(o * o, axis=-1, keepdims=True) + 1e-5) * g_ref[...] * (1.0 - lam_init)
    o_ref[...] = o.astype(o_ref.dtype)


def _diff_attn(h, lq1, lk1, lq2, lk2, sub_g, layer, t, tq=512):
    tq = min(tq, t)
    lam_init = 0.8 - 0.6 * math.exp(-0.3 * layer)
    kern = functools.partial(_diff_kernel, tq=tq, lam_init=lam_init)
    vec = pl.BlockSpec((1, LANE), lambda hh, i: (0, 0))
    qkb = C_QK // C_VDIM
    return pl.pallas_call(
        kern, grid=(C_HEADS, t // tq),
        in_specs=[pl.BlockSpec((tq, C_VDIM), lambda hh, i: (i, hh)),
                  pl.BlockSpec((t, C_VDIM), lambda hh, i: (0, qkb + hh)),
                  pl.BlockSpec((t, C_VDIM), lambda hh, i: (0, 2 * qkb + hh)),
                  vec, vec, vec, vec, pl.BlockSpec((1, C_VDIM), lambda hh, i: (0, 0))],
        out_specs=pl.BlockSpec((tq, C_VDIM), lambda hh, i: (i, hh)),
        out_shape=jax.ShapeDtypeStruct((t, C_V), BF16),
        compiler_params=_cparams(("parallel", "arbitrary")), name="diff_attn",
    )(h, h, h, lq1.reshape(1, LANE).astype(F32), lk1.reshape(1, LANE).astype(F32),
      lq2.reshape(1, LANE).astype(F32), lk2.reshape(1, LANE).astype(F32),
      sub_g.reshape(1, C_VDIM).astype(F32))


def _router_kernel(x_ref, w_ref, b_ref, comb_ref):
    logits = jnp.dot(x_ref[...], w_ref[...], preferred_element_type=F32,
                     precision=lax.Precision.HIGHEST) + b_ref[...]
    tm, ne = logits.shape
    jf = lax.broadcasted_iota(jnp.int32, (1, ne), 1).astype(F32)
    val = logits
    tops, hits = [], []
    for _ in range(TOP_K):
        m = jnp.max(val, axis=-1, keepdims=True)
        first = jnp.min(jnp.where(val == m, jf, float(ne)), axis=-1, keepdims=True)
        hit = jf == first
        tops.append(m)
        hits.append(hit)
        val = jnp.where(hit, -3e38, val)
    es = [jnp.exp(x - tops[0]) for x in tops]
    den = functools.reduce(lambda a, b: a + b, es)
    comb = jnp.zeros((tm, ne), F32)
    for e, hit in zip(es, hits):
        comb = jnp.where(hit, e / den, comb)
    comb_ref[...] = comb


def _router(x, rw, rb, tm=256):
    t, d = x.shape
    tm = min(tm, t)
    rw_p = jnp.zeros((d, LANE), F32).at[:, :N_EXPERTS].set(rw)
    rb_p = jnp.full((1, LANE), NEG_INF, F32).at[0, :N_EXPERTS].set(rb)
    return pl.pallas_call(
        _router_kernel, grid=(t // tm,),
        in_specs=[pl.BlockSpec((tm, d), lambda i: (i, 0)),
                  pl.BlockSpec((d, LANE), lambda i: (0, 0)),
                  pl.BlockSpec((1, LANE), lambda i: (0, 0))],
        out_specs=pl.BlockSpec((tm, LANE), lambda i: (i, 0)),
        out_shape=jax.ShapeDtypeStruct((t, LANE), F32),
        compiler_params=_cparams(("parallel",)), name="moe_router",
    )(x, rw_p, rb_p)


def _moe_kernel(x_ref, comb_ref, wgu_ref, bgu_ref, wd_ref, bd_ref, y_ref):
    e = pl.program_id(1)
    comb = comb_ref[...]

    @pl.when(e == 0)
    def _():
        y_ref[...] = jnp.dot(comb, bd_ref[...], preferred_element_type=F32,
                             precision=lax.Precision.HIGHEST)

    lane = lax.broadcasted_iota(jnp.int32, comb.shape, 1)
    w_e = jnp.sum(jnp.where(lane == e, comb, 0.0), axis=-1, keepdims=True)
    hgu = jnp.dot(x_ref[...], wgu_ref[...], preferred_element_type=F32) + bgu_ref[...]
    glu = jnp.minimum(hgu[:, :D_EXPERT], SWIGLU_LIMIT)
    lin = jnp.clip(hgu[:, D_EXPERT:], -SWIGLU_LIMIT, SWIGLU_LIMIT)
    act = glu * jax.nn.sigmoid(SWIGLU_ALPHA * glu) * (lin + 1.0) * w_e
    y_ref[...] += jnp.dot(act.astype(BF16), wd_ref[...], preferred_element_type=F32)


def _moe_dense(xb, comb, wgu, bgu, wd, bd, tm=512):
    t, d = xb.shape
    tm = min(tm, t)
    bd_p = jnp.zeros((LANE, d), F32).at[:N_EXPERTS].set(bd)
    return pl.pallas_call(
        _moe_kernel, grid=(t // tm, N_EXPERTS),
        in_specs=[pl.BlockSpec((tm, d), lambda i, e: (i, 0)),
                  pl.BlockSpec((tm, LANE), lambda i, e: (i, 0)),
                  pl.BlockSpec((None, d, 2 * D_EXPERT), lambda i, e: (e, 0, 0)),
                  pl.BlockSpec((None, 1, 2 * D_EXPERT), lambda i, e: (e, 0, 0)),
                  pl.BlockSpec((None, D_EXPERT, d), lambda i, e: (e, 0, 0)),
                  pl.BlockSpec((LANE, d), lambda i, e: (0, 0))],
        out_specs=pl.BlockSpec((tm, d), lambda i, e: (i, 0)),
        out_shape=jax.ShapeDtypeStruct((t, d), F32),
        compiler_params=_cparams(("parallel", "arbitrary")), name="moe_experts",
    )(xb, comb, wgu, bgu, wd, bd_p)


def _moe_ffn(xf, xb, rw, rb, wgu, bgu, wd, bd):
    comb = _router(xf, rw, rb)
    wgu_r = jnp.concatenate([wgu[..., 0::2], wgu[..., 1::2]], axis=-1).astype(BF16)
    bgu_r = jnp.concatenate([bgu[..., 0::2], bgu[..., 1::2]], axis=-1)[:, None, :].astype(F32)
    return _moe_dense(xb, comb, wgu_r, bgu_r, wd.astype(BF16), bd.astype(F32))


def _compress(kv, pe, w1, b1, w2, b2, t, rope_tab):
    g = A_KV_GROUPS
    nch = t // CMP_STRIDE
    n_pad = nch
    chunks = kv.reshape(nch, CMP_STRIDE, g, HEAD_DIM)
    blocks = jnp.concatenate([chunks[:-1], chunks[1:]], axis=1)
    flat = blocks.transpose(2, 0, 1, 3).reshape(g, nch - 1, CMP_LEN * HEAD_DIM)
    flat = jnp.pad(flat, ((0, 0), (0, 1), (0, 0))).reshape(g * n_pad, CMP_LEN * HEAD_DIM)
    w1b = w1.astype(BF16)
    pe_rows = jnp.zeros((8, CMP_LEN * HEAD_DIM), F32).at[0].set(pe.reshape(-1)).astype(BF16)
    pe_term = _mm(pe_rows, w1b, out_dtype=F32, name="cmp_pe")[0]
    hid = _mm(flat, w1b, bias=pe_term + b1, act="gelu", name="cmp_mlp1")
    if rope_tab is None:
        return _mm(hid, w2.astype(BF16), bias=b2, tn=LANE, name="cmp_mlp2")
    return _mm(hid, w2.astype(BF16), bias=b2, tn=LANE, rope_tabs=rope_tab[None],
               rope_modes=jnp.ones((1,), jnp.int32), name="cmp_mlp2_rope")


def _even_mixer(xb, w_in, w_out, cmpk, cmpv, tabs):
    t, d = xb.shape
    w = jnp.concatenate([w_in[:, :A_Q + 6 * A_KV],
                         w_in[:, A_Q + 6 * A_KV + A_GATE:],
                         w_in[:, A_Q + 6 * A_KV:A_Q + 6 * A_KV + A_GATE],
                         jnp.zeros((d, GATE_PAD - A_GATE), w_in.dtype)], axis=1).astype(BF16)
    tile_modes = np.zeros((EVEN_COLS // MM_TN,), np.int32)
    per = MM_TN // LANE
    for cb, nblk, mode in ((CB_QA, A_Q // LANE, 2), (CB_KS, A_KV // LANE, 1), (CB_KW, A_KV // LANE, 1),
                           (CB_QB, B_W // LANE, 2), (CB_KB, B_W // LANE, 1)):
        tile_modes[cb // per:(cb + nblk) // per] = mode
    h = _mm(xb, w, rope_tabs=tabs["qk"], rope_modes=jnp.asarray(tile_modes), name="even_in_proj")

    n_pad = t // CMP_STRIDE
    kc = _compress(h[:, CB_KC * LANE:CB_VC * LANE], *cmpk, t, jnp.tile(tabs["cmp"], (A_KV_GROUPS, 1)))
    vc = _compress(h[:, CB_VC * LANE:CB_KS * LANE], *cmpv, t, None)
    o_c, sel = _cmp_attn(h, kc, vc, t)
    o_s = _sel_attn(h, sel, t)

    tqw = min(WIN_LEN, t)
    (o_w,) = _swa(h, h, h, n_r=A_KV_GROUPS, n_tiles=t // tqw, tq=tqw, nh=A_HPG, shared_kv=True,
                  max_dist=WIN_LEN - 1, q_map=lambda r: r, k_map=lambda r: CB_KW + r,
                  v_map=lambda r: CB_VW + r, out_cols=A_Q, o_map=lambda r: r, with_lse=False,
                  name="nsa_window")

    per_row = EVEN_COLS // (B_HPG * LANE)
    ob_parts, lse_parts = [], []
    for gi, (window, dil) in enumerate(B_DILATIONS):
        tqd = min(128, t // dil)
        hv = h.reshape(t // dil, dil * EVEN_COLS)
        qb0 = CB_QB // B_HPG + gi
        kb0 = CB_KB // B_HPG + gi
        vb0 = CB_VB // B_HPG + gi
        o, lse = _swa(hv, hv, hv, n_r=dil, n_tiles=t // dil // tqd, tq=tqd, nh=B_HPG, shared_kv=False,
                      max_dist=window // dil, q_map=lambda r, b=qb0: r * per_row + b,
                      k_map=lambda r, b=kb0: r * per_row + b, v_map=lambda r, b=vb0: r * per_row + b,
                      out_cols=dil * B_HPG * LANE, o_map=lambda r: r, with_lse=True,
                      name="dilated_%d" % dil)
        ob_parts.append(o.reshape(t, B_HPG * LANE))
        lse_parts.append(lse.reshape(t, B_HPG * LANE))
    o_b = jnp.concatenate(ob_parts, axis=1)
    lse_b = jnp.concatenate(lse_parts, axis=1)
    mix_in = _mixprep(o_c, o_s, o_w, h, o_b, lse_b, t)
    return _mm(mix_in, w_out.astype(BF16), out_dtype=F32, name="even_out_proj")


def _odd_mixer(xb, w_in, w_out, lq1, lk1, lq2, lk2, sub_g, layer, tabs):
    t, d = xb.shape
    per = MM_TN // LANE
    tile_modes = np.zeros(((2 * C_QK + C_V) // MM_TN,), np.int32)
    tile_modes[:C_QK // MM_TN] = 2
    tile_modes[C_QK // MM_TN:2 * C_QK // MM_TN] = 1
    h = _mm(xb, w_in.astype(BF16), rope_tabs=tabs["qk"], rope_modes=jnp.asarray(tile_modes),
            name="odd_in_proj")
    o = _diff_attn(h, lq1, lk1, lq2, lk2, sub_g, layer, t)
    return _mm(o, w_out.astype(BF16), out_dtype=F32, name="odd_out_proj")


def kernel(x, even_w_in, even_w_out, cmpk_pe, cmpk_w1, cmpk_b1, cmpk_w2, cmpk_b2, cmpv_pe, cmpv_w1, cmpv_b1, cmpv_w2, cmpv_b2, odd_w_in, odd_w_out, lam_q1, lam_k1, lam_q2, lam_k2, subln_g, ln_mix_g, ln_mix_b, ln_ffn_g, ln_ffn_b, router_w, router_b, exp_w_gu, exp_b_gu, exp_w_down, exp_b_down):
    bsz, t, d = x.shape
    assert bsz == 1
    xf = x.reshape(t, d)
    xb = xf.astype(BF16)
    pos = jnp.arange(t, dtype=jnp.int32)
    cmp_end = jnp.arange(t // CMP_STRIDE, dtype=jnp.int32) * CMP_STRIDE + (CMP_LEN - 1)
    tabs = {
        "qk": jnp.stack([_rope_tables(pos, 1.0), _rope_tables(pos, HEAD_DIM ** -0.5)]),
        "cmp": _rope_tables(cmp_end, 1.0),
    }
    for layer in range(DEPTH):
        if layer % 2 == 0:
            e = layer // 2
            mix = _even_mixer(xb, even_w_in[e], even_w_out[e],
                              (cmpk_pe[e], cmpk_w1[e], cmpk_b1[e], cmpk_w2[e], cmpk_b2[e]),
                              (cmpv_pe[e], cmpv_w1[e], cmpv_b1[e], cmpv_w2[e], cmpv_b2[e]), tabs)
        else:
            o = layer // 2
            mix = _odd_mixer(xb, odd_w_in[o], odd_w_out[o], lam_q1[o], lam_k1[o], lam_q2[o],
                             lam_k2[o], subln_g[o], layer, tabs)
        xf, xb = _ln_res(xf, mix, ln_mix_g[layer], ln_mix_b[layer])
        ffn = _moe_ffn(xf, xb, router_w[layer], router_b[layer], exp_w_gu[layer], exp_b_gu[layer],
                       exp_w_down[layer], exp_b_down[layer])
        xf, xb = _ln_res(xf, ffn, ln_ffn_g[layer], ln_ffn_b[layer])
    return xf.reshape(bsz, t, d)
```

```python
import functools
import math

import jax
import jax.numpy as jnp
import numpy as np
from jax import lax
from jax.experimental import pallas as pl
from jax.experimental.pallas import tpu as pltpu

F32 = jnp.float32
BF16 = jnp.bfloat16

DEPTH = 4
HEAD_DIM = 128
ROPE_THETA = 500000.0
ROT_DIM = HEAD_DIM // 4
ROT_HALF = ROT_DIM // 2
NEG_INF = -1e30
SEL_FORCE = 1e9

A_HEADS = 20
A_KV_GROUPS = 4
A_HPG = A_HEADS // A_KV_GROUPS
CMP_LEN = 32
CMP_STRIDE = 16
CMP_HIDDEN = 4 * HEAD_DIM
SLC_LEN = 64
SLC_TOPK = 16
WIN_LEN = 512

B_DILATIONS = ((128, 1), (512, 4), (2048, 16))
B_HPG = 4
B_HEADS = B_HPG * len(B_DILATIONS)

C_HEADS = 16
C_VDIM = 2 * HEAD_DIM

N_EXPERTS = 32
TOP_K = 4
D_EXPERT = 384
SWIGLU_LIMIT = 7.0
SWIGLU_ALPHA = 1.702

DEEPNORM_ALPHA = (2.0 * DEPTH) ** 0.25

A_Q = A_HEADS * HEAD_DIM
A_KV = A_KV_GROUPS * HEAD_DIM
A_GATE = A_HEADS * 3
B_W = B_HEADS * HEAD_DIM
C_QK = 2 * C_HEADS * HEAD_DIM
C_V = C_HEADS * C_VDIM

LANE = 128
VMEM_LIMIT = 56 * 1024 * 1024

GATE_PAD = 512
EVEN_COLS = A_Q + 6 * A_KV + 3 * B_W + GATE_PAD
CB_QA = 0
CB_KC = A_Q // LANE
CB_VC = CB_KC + A_KV // LANE
CB_KS = CB_VC + A_KV // LANE
CB_VS = CB_KS + A_KV // LANE
CB_KW = CB_VS + A_KV // LANE
CB_VW = CB_KW + A_KV // LANE
CB_QB = CB_VW + A_KV // LANE
CB_KB = CB_QB + B_W // LANE
CB_VB = CB_KB + B_W // LANE
CB_GATE = CB_VB + B_W // LANE

MM_TN = 512


def _cparams(sem):
    return pltpu.CompilerParams(dimension_semantics=sem, vmem_limit_bytes=VMEM_LIMIT)


def _nt_dot(a, b):
    return lax.dot_general(a, b, (((1,), (1,)), ((), ())), preferred_element_type=F32)


def _rope_lanes(x, tab):
    c = tab[:, 0:LANE]
    sa = tab[:, LANE:2 * LANE]
    sb = tab[:, 2 * LANE:3 * LANE]
    return (x * c + pltpu.roll(x, LANE - ROT_HALF, axis=1) * sa
            + pltpu.roll(x, ROT_HALF, axis=1) * sb)


def _mm_kernel(*refs, has_bias, act, has_rope, n_sub):
    if has_rope:
        modes_ref, a_ref, b_ref = refs[0], refs[1], refs[2]
        rest = refs[3:]
    else:
        a_ref, b_ref = refs[0], refs[1]
        rest = refs[2:]
    idx = 0
    if has_bias:
        bias_ref = rest[idx]
        idx += 1
    if has_rope:
        tab_ref = rest[idx]
        idx += 1
    o_ref = rest[idx]

    acc = jnp.dot(a_ref[...], b_ref[...], preferred_element_type=F32)
    if has_bias:
        acc = acc + bias_ref[...]
    if act == "gelu":
        acc = jax.nn.gelu(acc, approximate=True)
    if not has_rope:
        o_ref[...] = acc.astype(o_ref.dtype)
        return
    mode = modes_ref[pl.program_id(1)]

    @pl.when(mode == 0)
    def _():
        o_ref[...] = acc.astype(o_ref.dtype)

    @pl.when(mode != 0)
    def _():
        tab = tab_ref[...]
        for s in range(n_sub):
            sl = slice(s * LANE, (s + 1) * LANE)
            o_ref[:, sl] = _rope_lanes(acc[:, sl], tab).astype(o_ref.dtype)


def _mm(a, b, *, bias=None, act=None, rope_tabs=None, rope_modes=None,
        out_dtype=None, tm=1024, tn=MM_TN, name="mm"):
    out_dtype = BF16 if out_dtype is None else out_dtype
    m, k = a.shape
    k2, n = b.shape
    assert k == k2
    tm = min(tm, m)
    tn = min(tn, n)
    assert m % tm == 0 and n % tn == 0
    has_bias = bias is not None
    has_rope = rope_tabs is not None
    grid = (m // tm, n // tn)
    kern = functools.partial(_mm_kernel, has_bias=has_bias, act=act, has_rope=has_rope,
                             n_sub=tn // LANE)
    if has_rope:
        in_specs = [pl.BlockSpec((tm, k), lambda i, j, md: (i, 0)),
                    pl.BlockSpec((k, tn), lambda i, j, md: (0, j))]
        args = [a, b]
        if has_bias:
            in_specs.append(pl.BlockSpec((1, tn), lambda i, j, md: (0, j)))
            args.append(bias.reshape(1, n).astype(F32))
        in_specs.append(pl.BlockSpec((None, tm, 3 * LANE),
                                     lambda i, j, md: (jnp.maximum(md[j] - 1, 0), i, 0)))
        args.append(rope_tabs)
        gs = pltpu.PrefetchScalarGridSpec(
            num_scalar_prefetch=1, grid=grid, in_specs=in_specs,
            out_specs=pl.BlockSpec((tm, tn), lambda i, j, md: (i, j)))
        return pl.pallas_call(kern, grid_spec=gs,
                              out_shape=jax.ShapeDtypeStruct((m, n), out_dtype),
                              compiler_params=_cparams(("parallel", "arbitrary")),
                              name=name)(rope_modes, *args)
    in_specs = [pl.BlockSpec((tm, k), lambda i, j: (i, 0)),
                pl.BlockSpec((k, tn), lambda i, j: (0, j))]
    args = [a, b]
    if has_bias:
        in_specs.append(pl.BlockSpec((1, tn), lambda i, j: (0, j)))
        args.append(bias.reshape(1, n).astype(F32))
    return pl.pallas_call(kern, grid=grid, in_specs=in_specs,
                          out_specs=pl.BlockSpec((tm, tn), lambda i, j: (i, j)),
                          out_shape=jax.ShapeDtypeStruct((m, n), out_dtype),
                          compiler_params=_cparams(("parallel", "arbitrary")),
                          name=name)(*args)


def _ln_kernel(x_ref, y_ref, g_ref, b_ref, of_ref, ob_ref):
    z = DEEPNORM_ALPHA * x_ref[...] + y_ref[...].astype(F32)
    mu = jnp.mean(z, axis=-1, keepdims=True)
    zc = z - mu
    var = jnp.mean(zc * zc, axis=-1, keepdims=True)
    out = zc * lax.rsqrt(var + 1e-5) * g_ref[...] + b_ref[...]
    of_ref[...] = out
    ob_ref[...] = out.astype(BF16)


def _ln_res(x, y, g, b, tm=256):
    t, d = x.shape
    tm = min(tm, t)
    row = pl.BlockSpec((tm, d), lambda i: (i, 0))
    vec = pl.BlockSpec((1, d), lambda i: (0, 0))
    return pl.pallas_call(
        _ln_kernel, grid=(t // tm,), in_specs=[row, row, vec, vec], out_specs=[row, row],
        out_shape=[jax.ShapeDtypeStruct((t, d), F32), jax.ShapeDtypeStruct((t, d), BF16)],
        compiler_params=_cparams(("parallel",)), name="ln_res",
    )(x, y, g.reshape(1, d), b.reshape(1, d))


def _rope_tables(pos, scale):
    inv = ROPE_THETA ** (-jnp.arange(ROT_HALF, dtype=F32) / ROT_HALF)
    ang = pos.astype(F32)[:, None] * inv[None, :]
    cos = jnp.cos(ang)
    sin = jnp.sin(ang)
    n = pos.shape[0]
    ones = jnp.ones((n, HEAD_DIM - ROT_DIM), F32)
    zer = jnp.zeros((n, HEAD_DIM - ROT_HALF), F32)
    c = jnp.concatenate([cos, cos, ones], axis=1)
    sa = jnp.concatenate([-sin, zer], axis=1)
    sb = jnp.concatenate([jnp.zeros((n, ROT_HALF), F32), sin,
                          jnp.zeros((n, HEAD_DIM - ROT_DIM), F32)], axis=1)
    return jnp.concatenate([c, sa, sb], axis=1) * scale


def _swa_kernel(q_ref, kp_ref, kc_ref, vp_ref, vc_ref, *out_refs, nh, shared_kv, max_dist,
                tq, with_lse):
    o_ref = out_refs[0]
    i = pl.program_id(1)
    row = lax.broadcasted_iota(jnp.int32, (tq, 2 * tq), 0)
    col = lax.broadcasted_iota(jnp.int32, (tq, 2 * tq), 1)
    dist = row + tq - col
    mask = (dist >= 0) & (dist <= max_dist) & (col + (i - 1) * tq >= 0)
    for h in range(nh):
        kv = 0 if shared_kv else h
        hs = slice(h * LANE, (h + 1) * LANE)
        ks = slice(kv * LANE, (kv + 1) * LANE)
        q = q_ref[:, hs]
        k = jnp.concatenate([kp_ref[:, ks], kc_ref[:, ks]], axis=0)
        v = jnp.concatenate([vp_ref[:, ks], vc_ref[:, ks]], axis=0)
        s = jnp.where(mask, _nt_dot(q, k), NEG_INF)
        m = jnp.max(s, axis=-1, keepdims=True)
        e = jnp.exp(s - m)
        den = jnp.sum(e, axis=-1, keepdims=True)
        o = jnp.dot(e.astype(BF16), v, preferred_element_type=F32) / den
        o_ref[:, hs] = o.astype(o_ref.dtype)
        if with_lse:
            out_refs[1][:, hs] = jnp.broadcast_to(m + jnp.log(den), (tq, LANE))


def _swa(qsrc, ksrc, vsrc, *, n_r, n_tiles, tq, nh, shared_kv, max_dist, q_map, k_map, v_map,
         out_cols, o_map, with_lse, name):
    kvw = LANE if shared_kv else nh * LANE
    qw = nh * LANE
    prev = lambda f: (lambda r, i: (jnp.maximum(i - 1, 0), f(r)))
    cur = lambda f: (lambda r, i: (i, f(r)))
    in_specs = [pl.BlockSpec((tq, qw), cur(q_map)),
                pl.BlockSpec((tq, kvw), prev(k_map)), pl.BlockSpec((tq, kvw), cur(k_map)),
                pl.BlockSpec((tq, kvw), prev(v_map)), pl.BlockSpec((tq, kvw), cur(v_map))]
    rows = n_tiles * tq
    out_shape = [jax.ShapeDtypeStruct((rows, out_cols), BF16)]
    out_specs = [pl.BlockSpec((tq, qw), cur(o_map))]
    if with_lse:
        out_shape.append(jax.ShapeDtypeStruct((rows, out_cols), F32))
        out_specs.append(pl.BlockSpec((tq, qw), cur(o_map)))
    kern = functools.partial(_swa_kernel, nh=nh, shared_kv=shared_kv, max_dist=max_dist, tq=tq,
                             with_lse=with_lse)
    return pl.pallas_call(kern, grid=(n_r, n_tiles), in_specs=in_specs, out_specs=out_specs,
                          out_shape=out_shape,
                          compiler_params=_cparams(("parallel", "arbitrary")),
                          name=name)(qsrc, ksrc, ksrc, vsrc, vsrc)


def _band_kernel(q_ref, k_ref, v_ref, o_ref, lse_ref, *, tq, window, dil):
    i = pl.program_id(1)
    q = q_ref[...]
    s0 = i * tq
    qpos = s0 + lax.broadcasted_iota(jnp.int32, (tq, 1), 0)
    kcol = lax.broadcasted_iota(jnp.int32, (1, tq), 1)
    first = jnp.maximum(s0 - window, 0) // tq

    def body(t, carry):
        m, l, acc = carry
        start = pl.multiple_of(t * tq, tq)
        k = k_ref[pl.ds(start, tq), :]
        v = v_ref[pl.ds(start, tq), :]
        dist = qpos - (kcol + t * tq)
        valid = (dist >= 0) & (dist <= window) & ((dist & (dil - 1)) == 0)
        s = jnp.where(valid, _nt_dot(q, k), NEG_INF)
        m_new = jnp.maximum(m, jnp.max(s, axis=-1, keepdims=True))
        alpha = jnp.exp(m - m_new)
        p = jnp.where(valid, jnp.exp(s - m_new), 0.0)
        l = alpha * l + jnp.sum(p, axis=-1, keepdims=True)
        acc = alpha * acc + jnp.dot(p.astype(BF16), v, preferred_element_type=F32)
        return m_new, l, acc

    init = (jnp.full((tq, 1), NEG_INF, F32), jnp.zeros((tq, 1), F32), jnp.zeros((tq, LANE), F32))
    m, l, acc = lax.fori_loop(first, i + 1, body, init)
    o_ref[...] = (acc / l).astype(o_ref.dtype)
    lse_ref[...] = jnp.broadcast_to(m + jnp.log(l), (tq, LANE))


def _band_attn(h, t, *, q_cb, k_cb, v_cb, nh, window, dil, tq=256):
    tq = min(tq, t)
    assert dil & (dil - 1) == 0
    kern = functools.partial(_band_kernel, tq=tq, window=window, dil=dil)
    blk = pl.BlockSpec((tq, LANE), lambda hd, i: (i, hd))
    return pl.pallas_call(
        kern, grid=(nh, t // tq),
        in_specs=[pl.BlockSpec((tq, LANE), lambda hd, i: (i, q_cb + hd)),
                  pl.BlockSpec((t, LANE), lambda hd, i: (0, k_cb + hd)),
                  pl.BlockSpec((t, LANE), lambda hd, i: (0, v_cb + hd))],
        out_specs=[blk, blk],
        out_shape=[jax.ShapeDtypeStruct((t, nh * LANE), BF16), jax.ShapeDtypeStruct((t, nh * LANE), F32)],
        compiler_params=_cparams(("parallel", "arbitrary")), name="dilated_band_%d" % dil,
    )(h, h, h)


def _cmp_kernel(q_ref, kc_ref, vc_ref, ov_ref, oc_ref, sel_ref, *, tq, n_cmp_pad, n_slc):
    i = pl.program_id(1)
    tqv = i * tq + lax.broadcasted_iota(jnp.int32, (tq, 1), 0)
    cmp_end = CMP_STRIDE * lax.broadcasted_iota(jnp.int32, (1, n_cmp_pad), 1) + (CMP_LEN - 1)
    maskc = cmp_end <= tqv
    kc = kc_ref[...]
    vc = vc_ref[...]
    psum = jnp.zeros((tq, n_cmp_pad), F32)
    for h in range(A_HPG):
        hs = slice(h * LANE, (h + 1) * LANE)
        s = jnp.where(maskc, _nt_dot(q_ref[:, hs], kc), NEG_INF)
        m = jnp.max(s, axis=-1, keepdims=True)
        e = jnp.where(maskc, jnp.exp(s - m), 0.0)
        den = jnp.sum(e, axis=-1, keepdims=True)
        p = e / jnp.maximum(den, 1e-30)
        oc_ref[:, hs] = jnp.dot(p.astype(BF16), vc, preferred_element_type=F32).astype(oc_ref.dtype)
        psum = psum + p
    ov = ov_ref[...]
    p1 = psum.astype(BF16)
    r1 = psum - p1.astype(F32)
    p2 = r1.astype(BF16)
    p3 = (r1 - p2.astype(F32)).astype(BF16)
    imp = (jnp.dot(p1, ov, preferred_element_type=F32) + jnp.dot(p2, ov, preferred_element_type=F32)
           + jnp.dot(p3, ov, preferred_element_type=F32))
    j = lax.broadcasted_iota(jnp.int32, (1, n_slc), 1)
    jf = j.astype(F32)
    tb = tqv // SLC_LEN
    forced = (j == 0) | (j == tb) | (j == tb - 1)
    val = jnp.where(forced, SEL_FORCE, jnp.where(j <= tb, imp, -SEL_FORCE))
    sel = jnp.zeros((tq, n_slc), F32)
    for _ in range(min(SLC_TOPK, n_slc)):
        m = jnp.max(val, axis=-1, keepdims=True)
        first = jnp.min(jnp.where(val == m, jf, float(n_slc)), axis=-1, keepdims=True)
        hit = jf == first
        sel = jnp.where(hit & (m > -0.5 * SEL_FORCE), 1.0, sel)
        val = jnp.where(hit, -3e38, val)
    sel_ref[...] = sel.astype(sel_ref.dtype)


def _cmp_attn(h, kc, vc, t, tq=128):
    n_cmp_pad = kc.shape[0] // A_KV_GROUPS
    n_slc = t // SLC_LEN
    ci = np.arange(n_cmp_pad)[:, None] * CMP_STRIDE
    sj = np.arange(n_slc)[None, :] * SLC_LEN
    overlap = ((ci < sj + SLC_LEN) & (ci + CMP_LEN > sj) & (np.arange(n_cmp_pad)[:, None] < t // CMP_STRIDE - 1))
    overlap = jnp.asarray(overlap.astype(np.float32), BF16)
    kern = functools.partial(_cmp_kernel, tq=tq, n_cmp_pad=n_cmp_pad, n_slc=n_slc)
    qw = A_HPG * LANE
    return pl.pallas_call(
        kern, grid=(A_KV_GROUPS, t // tq),
        in_specs=[pl.BlockSpec((tq, qw), lambda g, i: (i, g)),
                  pl.BlockSpec((n_cmp_pad, LANE), lambda g, i: (g, 0)),
                  pl.BlockSpec((n_cmp_pad, LANE), lambda g, i: (g, 0)),
                  pl.BlockSpec((n_cmp_pad, n_slc), lambda g, i: (0, 0))],
        out_specs=[pl.BlockSpec((tq, qw), lambda g, i: (i, g)),
                   pl.BlockSpec((tq, n_slc), lambda g, i: (i, g))],
        out_shape=[jax.ShapeDtypeStruct((t, A_Q), BF16),
                   jax.ShapeDtypeStruct((t, A_KV_GROUPS * n_slc), BF16)],
        compiler_params=_cparams(("parallel", "arbitrary")), name="cmp_attn_topk",
    )(h, kc, vc, overlap)


def _sel_kernel(q_ref, k_ref, v_ref, sel_ref, o_ref, *, tq, tk, n_slc):
    i = pl.program_id(1)
    nh = A_HPG
    q5 = jnp.concatenate([q_ref[:, h * LANE:(h + 1) * LANE] for h in range(nh)], axis=0)
    selb = sel_ref[...]
    tqv = i * tq + lax.broadcasted_iota(jnp.int32, (tq, 1), 0)
    blk_per_tile = tk // SLC_LEN
    ej = lax.broadcasted_iota(jnp.int32, (n_slc, tk), 0)
    ec = lax.broadcasted_iota(jnp.int32, (n_slc, tk), 1) // SLC_LEN
    kcol = lax.broadcasted_iota(jnp.int32, (1, tk), 1)
    n_kv = (i * tq + tq + tk - 1) // tk

    def body(t, carry):
        m, l, acc = carry
        start = pl.multiple_of(t * tk, tk)
        k = k_ref[pl.ds(start, tk), :]
        v = v_ref[pl.ds(start, tk), :]
        expand = jnp.where(ej == ec + t * blk_per_tile, 1.0, 0.0).astype(BF16)
        picked = jnp.dot(selb, expand, preferred_element_type=F32)
        valid = (picked > 0.5) & (kcol + t * tk <= tqv)
        valid5 = jnp.concatenate([valid] * nh, axis=0)
        s = jnp.where(valid5, _nt_dot(q5, k), NEG_INF)
        m_new = jnp.maximum(m, jnp.max(s, axis=-1, keepdims=True))
        alpha = jnp.exp(m - m_new)
        p = jnp.exp(s - m_new)
        l = alpha * l + jnp.sum(p, axis=-1, keepdims=True)
        acc = alpha * acc + jnp.dot(p.astype(BF16), v, preferred_element_type=F32)
        return m_new, l, acc

    init = (jnp.full((nh * tq, 1), NEG_INF, F32), jnp.zeros((nh * tq, 1), F32),
            jnp.zeros((nh * tq, LANE), F32))
    m, l, acc = lax.fori_loop(0, n_kv, body, init)
    o = acc / l
    for h in range(nh):
        o_ref[:, h * LANE:(h + 1) * LANE] = o[h * tq:(h + 1) * tq].astype(o_ref.dtype)


def _sel_attn(h, sel, t, tq=128, tk=512):
    n_slc = t // SLC_LEN
    tk = min(tk, t)
    qw = A_HPG * LANE
    kern = functools.partial(_sel_kernel, tq=tq, tk=tk, n_slc=n_slc)
    return pl.pallas_call(
        kern, grid=(A_KV_GROUPS, t // tq),
        in_specs=[pl.BlockSpec((tq, qw), lambda g, i: (i, g)),
                  pl.BlockSpec((t, LANE), lambda g, i: (0, CB_KS + g)),
                  pl.BlockSpec((t, LANE), lambda g, i: (0, CB_VS + g)),
                  pl.BlockSpec((tq, n_slc), lambda g, i: (i, g))],
        out_specs=pl.BlockSpec((tq, qw), lambda g, i: (i, g)),
        out_shape=jax.ShapeDtypeStruct((t, A_Q), BF16),
        compiler_params=_cparams(("parallel", "arbitrary")), name="sel_attn",
    )(h, h, h, sel)


def _mixprep_kernel(oc_ref, os_ref, ow_ref, gate_ref, ob_ref, lse_ref, out_ref):
    gate = jax.nn.sigmoid(gate_ref[...].astype(F32))
    for h in range(A_HEADS):
        hs = slice(h * LANE, (h + 1) * LANE)
        o = (gate[:, 3 * h:3 * h + 1] * oc_ref[:, hs].astype(F32)
             + gate[:, 3 * h + 1:3 * h + 2] * os_ref[:, hs].astype(F32)
             + gate[:, 3 * h + 2:3 * h + 3] * ow_ref[:, hs].astype(F32))
        out_ref[:, hs] = o.astype(out_ref.dtype)
    ng = len(B_DILATIONS)
    for hi in range(B_HPG):
        lses = [lse_ref[:, (g * B_HPG + hi) * LANE:(g * B_HPG + hi + 1) * LANE] for g in range(ng)]
        m = functools.reduce(jnp.maximum, lses)
        es = [jnp.exp(x - m) for x in lses]
        den = functools.reduce(lambda a, b: a + b, es)
        for g in range(ng):
            src = slice((g * B_HPG + hi) * LANE, (g * B_HPG + hi + 1) * LANE)
            dst = slice(A_Q + (g * B_HPG + hi) * LANE, A_Q + (g * B_HPG + hi + 1) * LANE)
            out_ref[:, dst] = (ob_ref[:, src].astype(F32) * (es[g] / den)).astype(out_ref.dtype)


def _mixprep(o_c, o_s, o_w, h, o_b, lse_b, t, tq=256):
    tq = min(tq, t)
    a = pl.BlockSpec((tq, A_Q), lambda i: (i, 0))
    b = pl.BlockSpec((tq, B_W), lambda i: (i, 0))
    return pl.pallas_call(
        _mixprep_kernel, grid=(t // tq,),
        in_specs=[a, a, a, pl.BlockSpec((tq, LANE), lambda i: (i, CB_GATE)), b, b],
        out_specs=pl.BlockSpec((tq, A_Q + B_W), lambda i: (i, 0)),
        out_shape=jax.ShapeDtypeStruct((t, A_Q + B_W), BF16),
        compiler_params=_cparams(("parallel",)), name="even_mixprep",
    )(o_c, o_s, o_w, h, o_b, lse_b)


def _diff_kernel(q_ref, k_ref, v_ref, lq1_ref, lk1_ref, lq2_ref, lk2_ref, g_ref, o_ref, *,
                 tq, tk, lam_init):
    i = pl.program_id(1)
    q1 = q_ref[:, 0:LANE]
    q2 = q_ref[:, LANE:2 * LANE]
    lam = (jnp.exp(jnp.sum(lq1_ref[...] * lk1_ref[...], axis=-1, keepdims=True))
           - jnp.exp(jnp.sum(lq2_ref[...] * lk2_ref[...], axis=-1, keepdims=True)) + lam_init)

    def tile(start, carry, diag_off):
        m1, l1, a1, m2, l2, a2 = carry
        k = k_ref[pl.ds(start, tk), :]
        v = v_ref[pl.ds(start, tk), :]
        s1 = _nt_dot(q1, k[:, 0:LANE])
        s2 = _nt_dot(q2, k[:, LANE:2 * LANE])
        if diag_off is not None:
            row = lax.broadcasted_iota(jnp.int32, (tq, tk), 0)
            col = lax.broadcasted_iota(jnp.int32, (tq, tk), 1) + diag_off
            s1 = jnp.where(col <= row, s1, NEG_INF)
            s2 = jnp.where(col <= row, s2, NEG_INF)
        out = []
        for s, m, l, a in ((s1, m1, l1, a1), (s2, m2, l2, a2)):
            m_new = jnp.maximum(m, jnp.max(s, axis=-1, keepdims=True))
            alpha = jnp.exp2(m - m_new)
            p = jnp.exp2(s - m_new)
            l = alpha * l + jnp.sum(p, axis=-1, keepdims=True)
            a = alpha * a + jnp.dot(p.astype(BF16), v, preferred_element_type=F32)
            out += [m_new, l, a]
        return tuple(out)

    init = (jnp.full((tq, 1), NEG_INF, F32), jnp.zeros((tq, 1), F32), jnp.zeros((tq, C_VDIM), F32)) * 2
    per = tq // tk
    carry = lax.fori_loop(0, i * per, lambda t, c: tile(pl.multiple_of(t * tk, tk), c, None), init)
    for u in range(per):
        carry = tile(pl.multiple_of(i * tq + u * tk, tk), carry, u * tk)
    m1, l1, a1, m2, l2, a2 = carry
    o = a1 / l1 - lam * (a2 / l2)
    o = o * lax.rsqrt(jnp.mean(o * o, axis=-1, keepdims=True) + 1e-5) * g_ref[...] * (1.0 - lam_init)
    o_ref[...] = o.astype(o_ref.dtype)


def _diff_attn(h, lq1, lk1, lq2, lk2, sub_g, layer, t, tq=512, tk=512):
    tq = min(tq, t)
    tk = min(tk, tq)
    lam_init = 0.8 - 0.6 * math.exp(-0.3 * layer)
    kern = functools.partial(_diff_kernel, tq=tq, tk=tk, lam_init=lam_init)
    vec = pl.BlockSpec((1, LANE), lambda hh, i: (0, 0))
    qkb = C_QK // C_VDIM
    return pl.pallas_call(
        kern, grid=(C_HEADS, t // tq),
        in_specs=[pl.BlockSpec((tq, C_VDIM), lambda hh, i: (i, hh)),
                  pl.BlockSpec((t, C_VDIM), lambda hh, i: (0, qkb + hh)),
                  pl.BlockSpec((t, C_VDIM), lambda hh, i: (0, 2 * qkb + hh)),
                  vec, vec, vec, vec, pl.BlockSpec((1, C_VDIM), lambda hh, i: (0, 0))],
        out_specs=pl.BlockSpec((tq, C_VDIM), lambda hh, i: (i, hh)),
        out_shape=jax.ShapeDtypeStruct((t, C_V), BF16),
        compiler_params=_cparams(("parallel", "arbitrary")), name="diff_attn",
    )(h, h, h, lq1.reshape(1, LANE).astype(F32), lk1.reshape(1, LANE).astype(F32),
      lq2.reshape(1, LANE).astype(F32), lk2.reshape(1, LANE).astype(F32),
      sub_g.reshape(1, C_VDIM).astype(F32))


ROUTE_EID = 0
ROUTE_RANK = 4
ROUTE_W = 8
MOE_TM = 256


def _router_kernel(x_ref, w_ref, b_ref, comb_ref, route_ref, cnt_ref, carry_ref):
    i = pl.program_id(0)

    @pl.when(i == 0)
    def _():
        carry_ref[...] = jnp.zeros_like(carry_ref)

    logits = jnp.dot(x_ref[...], w_ref[...], preferred_element_type=F32,
                     precision=lax.Precision.HIGHEST) + b_ref[...]
    tm, ne = logits.shape
    lane = lax.broadcasted_iota(jnp.int32, (1, ne), 1)
    jf = lane.astype(F32)
    val = logits
    tops, hits, firsts = [], [], []
    for _ in range(TOP_K):
        m = jnp.max(val, axis=-1, keepdims=True)
        first = jnp.min(jnp.where(val == m, jf, float(ne)), axis=-1, keepdims=True)
        hit = jf == first
        tops.append(m)
        hits.append(hit)
        firsts.append(first)
        val = jnp.where(hit, -3e38, val)
    es = [jnp.exp(x - tops[0]) for x in tops]
    den = functools.reduce(lambda a, b: a + b, es)
    comb = jnp.zeros((tm, ne), F32)
    picked = jnp.zeros((tm, ne), F32)
    for e, hit in zip(es, hits):
        comb = jnp.where(hit, e / den, comb)
        picked = jnp.where(hit, 1.0, picked)
    comb_ref[...] = comb
    row = lax.broadcasted_iota(jnp.int32, (tm, tm), 0)
    col = lax.broadcasted_iota(jnp.int32, (tm, tm), 1)
    tri = jnp.where(row > col, 1.0, 0.0).astype(BF16)
    rank = jnp.dot(tri, picked.astype(BF16), preferred_element_type=F32) + carry_ref[...]
    carry_ref[...] += jnp.sum(picked, axis=0, keepdims=True)
    cnt_ref[...] = carry_ref[...]
    route = jnp.zeros((tm, ne), F32)
    for k in range(TOP_K):
        rk = jnp.sum(jnp.where(hits[k], rank, 0.0), axis=-1, keepdims=True)
        route = jnp.where(lane == ROUTE_EID + k, firsts[k], route)
        route = jnp.where(lane == ROUTE_RANK + k, rk, route)
        route = jnp.where(lane == ROUTE_W + k, es[k] / den, route)
    route_ref[...] = route


def _router(x, rw, rb, tm=256):
    t, d = x.shape
    tm = min(tm, t)
    rw_p = jnp.zeros((d, LANE), F32).at[:, :N_EXPERTS].set(rw)
    rb_p = jnp.full((1, LANE), NEG_INF, F32).at[0, :N_EXPERTS].set(rb)
    blk = pl.BlockSpec((tm, LANE), lambda i: (i, 0))
    return pl.pallas_call(
        _router_kernel, grid=(t // tm,),
        in_specs=[pl.BlockSpec((tm, d), lambda i: (i, 0)),
                  pl.BlockSpec((d, LANE), lambda i: (0, 0)),
                  pl.BlockSpec((1, LANE), lambda i: (0, 0))],
        out_specs=[blk, blk, pl.BlockSpec((1, LANE), lambda i: (0, 0))],
        out_shape=[jax.ShapeDtypeStruct((t, LANE), F32), jax.ShapeDtypeStruct((t, LANE), F32),
                   jax.ShapeDtypeStruct((1, LANE), F32)],
        scratch_shapes=[pltpu.VMEM((1, LANE), F32)],
        compiler_params=_cparams(("arbitrary",)), name="moe_router",
    )(x, rw_p, rb_p)


def _row_copy(src, src_row, dst, dst_row, sem):
    return pltpu.make_async_copy(src.at[pl.ds(src_row, 1)], dst.at[pl.ds(dst_row, 1)], sem)


def _dispatch_kernel(gend_ref, dest_ref, x_hbm, zeros_hbm, xs_hbm, sem, *, tm):
    i = pl.program_id(0)

    @pl.when(i == 0)
    def _():
        def clear(e):
            start = pl.multiple_of(gend_ref[e] - MOE_TM, MOE_TM)
            return pltpu.make_async_copy(zeros_hbm, xs_hbm.at[pl.ds(start, MOE_TM)], sem)
        for e in range(N_EXPERTS):
            clear(e).start()
        for e in range(N_EXPERTS):
            clear(e).wait()

        def clear_tail(j, c):
            cp = pltpu.make_async_copy(
                zeros_hbm, xs_hbm.at[pl.ds(pl.multiple_of(j * MOE_TM, MOE_TM), MOE_TM)], sem)
            cp.start()
            cp.wait()
            return c

        lax.fori_loop(gend_ref[N_EXPERTS - 1] // MOE_TM, xs_hbm.shape[0] // MOE_TM, clear_tail, 0)

    def issue(r, c):
        for k in range(TOP_K):
            _row_copy(x_hbm, i * tm + r, xs_hbm, dest_ref[TOP_K * r + k], sem).start()
        return c

    lax.fori_loop(0, tm, issue, 0)

    def drain(r, c):
        for k in range(TOP_K):
            _row_copy(x_hbm, 0, xs_hbm, 0, sem).wait()
        return c

    lax.fori_loop(0, tm, drain, 0)


def _dispatch(xf, dest_flat, gend, n_rows, tm=256):
    t, d = xf.shape
    tm = min(tm, t)
    gs = pltpu.PrefetchScalarGridSpec(
        num_scalar_prefetch=1, grid=(t // tm,),
        in_specs=[pl.BlockSpec((TOP_K * tm,), lambda i, ge: (i,), memory_space=pltpu.SMEM),
                  pl.BlockSpec(memory_space=pl.ANY), pl.BlockSpec(memory_space=pl.ANY)],
        out_specs=pl.BlockSpec(memory_space=pl.ANY),
        scratch_shapes=[pltpu.SemaphoreType.DMA(())])
    return pl.pallas_call(
        functools.partial(_dispatch_kernel, tm=tm), grid_spec=gs, out_shape=jax.ShapeDtypeStruct((n_rows, d), F32),
        compiler_params=_cparams(("arbitrary",)), name="moe_dispatch",
    )(gend, dest_flat, xf, jnp.zeros((MOE_TM, d), F32))


def _expert_kernel(eid_ref, nused_ref, xs_ref, wgu_ref, bgu_ref, wd_ref, pick_ref, ys_ref):
    j = pl.program_id(0)

    @pl.when(j < nused_ref[0])
    def _():
        x = xs_ref[...].astype(BF16)
        hgu = jnp.dot(x, wgu_ref[...], preferred_element_type=F32) + bgu_ref[...]
        glu = jnp.minimum(hgu, SWIGLU_LIMIT)
        gated = glu * jax.nn.sigmoid(SWIGLU_ALPHA * glu)
        lin = jnp.clip(hgu, -SWIGLU_LIMIT, SWIGLU_LIMIT) + 1.0
        prod = jnp.concatenate(
            [gated[:, s:s + LANE] * pltpu.roll(lin[:, s:s + LANE], LANE - 1, axis=1)
             for s in range(0, 2 * D_EXPERT, LANE)], axis=1)
        act = jnp.dot(prod.astype(BF16), pick_ref[...], preferred_element_type=F32)
        ys_ref[...] = jnp.dot(act.astype(BF16), wd_ref[...], preferred_element_type=F32)

    @pl.when(j >= nused_ref[0])
    def _():
        ys_ref[...] = jnp.zeros_like(ys_ref)


def _experts(xs, tile_eid, nused, wgu, bgu, wd):
    n_rows, d = xs.shape
    n_tiles = n_rows // MOE_TM
    pick = np.zeros((2 * D_EXPERT, D_EXPERT), np.float32)
    pick[2 * np.arange(D_EXPERT), np.arange(D_EXPERT)] = 1.0
    tile = lambda j, eid, nu: (jnp.minimum(j, nu[0] - 1), 0)
    gs = pltpu.PrefetchScalarGridSpec(
        num_scalar_prefetch=2, grid=(n_tiles,),
        in_specs=[pl.BlockSpec((MOE_TM, d), tile),
                  pl.BlockSpec((None, d, 2 * D_EXPERT), lambda j, eid, nu: (eid[j], 0, 0)),
                  pl.BlockSpec((None, 1, 2 * D_EXPERT), lambda j, eid, nu: (eid[j], 0, 0)),
                  pl.BlockSpec((None, D_EXPERT, d), lambda j, eid, nu: (eid[j], 0, 0)),
                  pl.BlockSpec((2 * D_EXPERT, D_EXPERT), lambda j, eid, nu: (0, 0))],
        out_specs=pl.BlockSpec((MOE_TM, d), lambda j, eid, nu: (j, 0)))
    return pl.pallas_call(
        _expert_kernel, grid_spec=gs, out_shape=jax.ShapeDtypeStruct((n_rows, d), F32),
        compiler_params=_cparams(("arbitrary",)), name="moe_experts",
    )(tile_eid, nused, xs, wgu, bgu, wd, jnp.asarray(pick, BF16))


def _combine_ln_kernel(dest_ref, x_ref, route_ref, comb_ref, bd_ref, g_ref, b_ref, ys_hbm,
                       of_ref, ob_ref, ybuf, sem, *, tm):
    def issue(r, c):
        for k in range(TOP_K):
            idx = TOP_K * r + k
            src_row = dest_ref[idx // LANE, idx % LANE]
            _row_copy(ys_hbm, src_row, ybuf.at[k], r, sem).start()
        return c

    lax.fori_loop(0, tm, issue, 0)
    y = jnp.dot(comb_ref[...], bd_ref[...], preferred_element_type=F32,
                precision=lax.Precision.HIGHEST)

    def drain(r, c):
        for k in range(TOP_K):
            _row_copy(ys_hbm, 0, ybuf.at[k], 0, sem).wait()
        return c

    lax.fori_loop(0, tm, drain, 0)
    route = route_ref[...]
    for k in range(TOP_K):
        y = y + route[:, ROUTE_W + k:ROUTE_W + k + 1] * ybuf[k]
    z = DEEPNORM_ALPHA * x_ref[...] + y
    mu = jnp.mean(z, axis=-1, keepdims=True)
    zc = z - mu
    var = jnp.mean(zc * zc, axis=-1, keepdims=True)
    out = zc * lax.rsqrt(var + 1e-5) * g_ref[...] + b_ref[...]
    of_ref[...] = out
    ob_ref[...] = out.astype(BF16)


def _combine_ln(xf, ys, dest_flat, route, comb, bd, g, b, tm=128):
    t, d = xf.shape
    tm = min(tm, t)
    bd_p = jnp.zeros((LANE, d), F32).at[:N_EXPERTS].set(bd)
    row = pl.BlockSpec((tm, d), lambda i: (i, 0))
    lanes = pl.BlockSpec((tm, LANE), lambda i: (i, 0))
    vec = pl.BlockSpec((1, d), lambda i: (0, 0))
    kern = functools.partial(_combine_ln_kernel, tm=tm)
    idx_rows = TOP_K * tm // LANE
    dest3 = dest_flat.reshape(t // tm, idx_rows, LANE)
    return pl.pallas_call(
        kern, grid=(t // tm,),
        in_specs=[pl.BlockSpec((None, idx_rows, LANE), lambda i: (i, 0, 0), memory_space=pltpu.SMEM),
                  row, lanes, lanes, pl.BlockSpec((LANE, d), lambda i: (0, 0)), vec, vec,
                  pl.BlockSpec(memory_space=pl.ANY)],
        out_specs=[row, row],
        out_shape=[jax.ShapeDtypeStruct((t, d), F32), jax.ShapeDtypeStruct((t, d), BF16)],
        scratch_shapes=[pltpu.VMEM((TOP_K, tm, d), F32), pltpu.SemaphoreType.DMA(())],
        compiler_params=_cparams(("arbitrary",)), name="moe_combine_ln",
    )(dest3, xf, route, comb, bd_p, g.reshape(1, d), b.reshape(1, d), ys)


def _moe_ln(xf, rw, rb, wgu, bgu, wd, bd, g, b):
    t, d = xf.shape
    comb, route, cnt = _router(xf, rw, rb)
    cnt = cnt[0, :N_EXPERTS].astype(jnp.int32)
    tiles_e = jnp.maximum((cnt + MOE_TM - 1) // MOE_TM, 1)
    gend = jnp.cumsum(tiles_e) * MOE_TM
    gstart = gend - tiles_e * MOE_TM
    eid = route[:, ROUTE_EID:ROUTE_EID + TOP_K].astype(jnp.int32)
    rank = route[:, ROUTE_RANK:ROUTE_RANK + TOP_K].astype(jnp.int32)
    onehot = eid[..., None] == jnp.arange(N_EXPERTS, dtype=jnp.int32)
    dest = jnp.sum(jnp.where(onehot, gstart, 0), axis=-1) + rank
    dest_flat = dest.reshape(-1).astype(jnp.int32)
    n_tiles = (t * TOP_K) // MOE_TM + N_EXPERTS
    tile_eid = jnp.minimum(jnp.searchsorted(gend, jnp.arange(n_tiles, dtype=jnp.int32) * MOE_TM,
                                            side="right"), N_EXPERTS - 1).astype(jnp.int32)
    nused = (gend[-1:] // MOE_TM).astype(jnp.int32)
    xs = _dispatch(xf, dest_flat, gend.astype(jnp.int32), n_tiles * MOE_TM)
    ys = _experts(xs, tile_eid, nused, wgu.astype(BF16), bgu[:, None, :].astype(F32), wd.astype(BF16))
    return _combine_ln(xf, ys, dest_flat, route, comb, bd.astype(F32), g, b)


def _compress(kv, pe, w1, b1, w2, b2, t, rope_tab):
    g = A_KV_GROUPS
    nch = t // CMP_STRIDE
    n_pad = nch
    chunks = kv.reshape(nch, CMP_STRIDE, g, HEAD_DIM)
    blocks = jnp.concatenate([chunks[:-1], chunks[1:]], axis=1)
    flat = blocks.transpose(2, 0, 1, 3).reshape(g, nch - 1, CMP_LEN * HEAD_DIM)
    flat = jnp.pad(flat, ((0, 0), (0, 1), (0, 0))).reshape(g * n_pad, CMP_LEN * HEAD_DIM)
    w1b = w1.astype(BF16)
    pe_rows = jnp.zeros((8, CMP_LEN * HEAD_DIM), F32).at[0].set(pe.reshape(-1)).astype(BF16)
    pe_term = _mm(pe_rows, w1b, out_dtype=F32, name="cmp_pe")[0]
    hid = _mm(flat, w1b, bias=pe_term + b1, act="gelu", name="cmp_mlp1")
    if rope_tab is None:
        return _mm(hid, w2.astype(BF16), bias=b2, tn=LANE, name="cmp_mlp2")
    return _mm(hid, w2.astype(BF16), bias=b2, tn=LANE, rope_tabs=rope_tab[None],
               rope_modes=jnp.ones((1,), jnp.int32), name="cmp_mlp2_rope")


def _even_w_kernel(a_ref, b_ref, o_ref, *, first_shifted, gate_tile):
    j = pl.program_id(1)
    a = a_ref[...]
    tn = a.shape[1]

    @pl.when(j < first_shifted)
    def _():
        o_ref[...] = a.astype(o_ref.dtype)

    @pl.when((j >= first_shifted) & (j < gate_tile))
    def _():
        b = b_ref[...]
        o_ref[...] = jnp.concatenate([a[:, A_GATE:], b[:, :A_GATE]], axis=1).astype(o_ref.dtype)

    @pl.when(j == gate_tile)
    def _():
        lane = lax.broadcasted_iota(jnp.int32, a.shape, 1)
        o_ref[...] = jnp.where(lane < A_GATE, a, 0.0).astype(o_ref.dtype)


def _even_w_layout(w_in, tr=512):
    d = w_in.shape[0]
    tr = min(tr, d)
    first_shifted = (A_Q + 6 * A_KV) // MM_TN
    gate_tile = EVEN_COLS // MM_TN - 1
    last_in = (w_in.shape[1] - 1) // MM_TN
    kern = functools.partial(_even_w_kernel, first_shifted=first_shifted, gate_tile=gate_tile)
    return pl.pallas_call(
        kern, grid=(d // tr, EVEN_COLS // MM_TN),
        in_specs=[pl.BlockSpec((tr, MM_TN), lambda i, j: (i, jnp.where(j == gate_tile, first_shifted, j))),
                  pl.BlockSpec((tr, MM_TN), lambda i, j: (i, jnp.minimum(j + 1, last_in)))],
        out_specs=pl.BlockSpec((tr, MM_TN), lambda i, j: (i, j)),
        out_shape=jax.ShapeDtypeStruct((d, EVEN_COLS), BF16),
        compiler_params=_cparams(("parallel", "arbitrary")), name="even_w_layout",
    )(w_in, w_in)


def _even_mixer(xb, w_in, w_out, cmpk, cmpv, tabs):
    t, d = xb.shape
    w = _even_w_layout(w_in)
    tile_modes = np.zeros((EVEN_COLS // MM_TN,), np.int32)
    per = MM_TN // LANE
    for cb, nblk, mode in ((CB_QA, A_Q // LANE, 2), (CB_KS, A_KV // LANE, 1), (CB_KW, A_KV // LANE, 1),
                           (CB_QB, B_W // LANE, 2), (CB_KB, B_W // LANE, 1)):
        tile_modes[cb // per:(cb + nblk) // per] = mode
    h = _mm(xb, w, rope_tabs=tabs["qk"], rope_modes=jnp.asarray(tile_modes), name="even_in_proj")

    n_pad = t // CMP_STRIDE
    kc = _compress(h[:, CB_KC * LANE:CB_VC * LANE], *cmpk, t, jnp.tile(tabs["cmp"], (A_KV_GROUPS, 1)))
    vc = _compress(h[:, CB_VC * LANE:CB_KS * LANE], *cmpv, t, None)
    o_c, sel = _cmp_attn(h, kc, vc, t)
    o_s = _sel_attn(h, sel, t)

    tqw = min(WIN_LEN, t)
    (o_w,) = _swa(h, h, h, n_r=A_KV_GROUPS, n_tiles=t // tqw, tq=tqw, nh=A_HPG, shared_kv=True,
                  max_dist=WIN_LEN - 1, q_map=lambda r: r, k_map=lambda r: CB_KW + r,
                  v_map=lambda r: CB_VW + r, out_cols=A_Q, o_map=lambda r: r, with_lse=False,
                  name="nsa_window")

    ob_parts, lse_parts = [], []
    for gi, (window, dil) in enumerate(B_DILATIONS):
        if dil == 1:
            tqd = min(max(window, LANE), t)
            o, lse = _swa(h, h, h, n_r=1, n_tiles=t // tqd, tq=tqd, nh=B_HPG, shared_kv=False,
                          max_dist=window, q_map=lambda r, b=CB_QB // B_HPG + gi: b,
                          k_map=lambda r, b=CB_KB // B_HPG + gi: b,
                          v_map=lambda r, b=CB_VB // B_HPG + gi: b,
                          out_cols=B_HPG * LANE, o_map=lambda r: r, with_lse=True, name="dilated_1")
        else:
            o, lse = _band_attn(h, t, q_cb=CB_QB + gi * B_HPG, k_cb=CB_KB + gi * B_HPG,
                                v_cb=CB_VB + gi * B_HPG, nh=B_HPG, window=window, dil=dil)
        ob_parts.append(o)
        lse_parts.append(lse)
    o_b = jnp.concatenate(ob_parts, axis=1)
    lse_b = jnp.concatenate(lse_parts, axis=1)
    mix_in = _mixprep(o_c, o_s, o_w, h, o_b, lse_b, t)
    return _mm(mix_in, w_out.astype(BF16), out_dtype=F32, name="even_out_proj")


def _odd_mixer(xb, w_in, w_out, lq1, lk1, lq2, lk2, sub_g, layer, tabs):
    t, d = xb.shape
    per = MM_TN // LANE
    tile_modes = np.zeros(((2 * C_QK + C_V) // MM_TN,), np.int32)
    tile_modes[:C_QK // MM_TN] = 3
    tile_modes[C_QK // MM_TN:2 * C_QK // MM_TN] = 1
    h = _mm(xb, w_in.astype(BF16), rope_tabs=tabs["qk"], rope_modes=jnp.asarray(tile_modes),
            name="odd_in_proj")
    o = _diff_attn(h, lq1, lk1, lq2, lk2, sub_g, layer, t)
    return _mm(o, w_out.astype(BF16), out_dtype=F32, name="odd_out_proj")


def kernel(x, even_w_in, even_w_out, cmpk_pe, cmpk_w1, cmpk_b1, cmpk_w2, cmpk_b2, cmpv_pe, cmpv_w1, cmpv_b1, cmpv_w2, cmpv_b2, odd_w_in, odd_w_out, lam_q1, lam_k1, lam_q2, lam_k2, subln_g, ln_mix_g, ln_mix_b, ln_ffn_g, ln_ffn_b, router_w, router_b, exp_w_gu, exp_b_gu, exp_w_down, exp_b_down):
    bsz, t, d = x.shape
    assert bsz == 1
    xf = x.reshape(t, d)
    xb = xf.astype(BF16)
    pos = jnp.arange(t, dtype=jnp.int32)
    cmp_end = jnp.arange(t // CMP_STRIDE, dtype=jnp.int32) * CMP_STRIDE + (CMP_LEN - 1)
    tabs = {
        "qk": jnp.stack([_rope_tables(pos, 1.0), _rope_tables(pos, HEAD_DIM ** -0.5),
                         _rope_tables(pos, HEAD_DIM ** -0.5 * math.log2(math.e))]),
        "cmp": _rope_tables(cmp_end, 1.0),
    }
    for layer in range(DEPTH):
        if layer % 2 == 0:
            e = layer // 2
            mix = _even_mixer(xb, even_w_in[e], even_w_out[e],
                              (cmpk_pe[e], cmpk_w1[e], cmpk_b1[e], cmpk_w2[e], cmpk_b2[e]),
                              (cmpv_pe[e], cmpv_w1[e], cmpv_b1[e], cmpv_w2[e], cmpv_b2[e]), tabs)
        else:
            o = layer // 2
            mix = _odd_mixer(xb, odd_w_in[o], odd_w_out[o], lam_q1[o], lam_k1[o], lam_q2[o],
                             lam_k2[o], subln_g[o], layer, tabs)
        xf, xb = _ln_res(xf, mix, ln_mix_g[layer], ln_mix_b[layer])
        xf, xb = _moe_ln(xf, router_w[layer], router_b[layer], exp_w_gu[layer], exp_b_gu[layer],
                         exp_w_down[layer], exp_b_down[layer], ln_ffn_g[layer], ln_ffn_b[layer])
    return xf.reshape(bsz, t, d)
```

```python
import functools
import math

import jax
import jax.numpy as jnp
import numpy as np
from jax import lax
from jax.experimental import pallas as pl
from jax.experimental.pallas import tpu as pltpu

F32 = jnp.float32
BF16 = jnp.bfloat16

DEPTH = 4
HEAD_DIM = 128
ROPE_THETA = 500000.0
ROT_DIM = HEAD_DIM // 4
ROT_HALF = ROT_DIM // 2
NEG_INF = -1e30
SEL_FORCE = 1e9

A_HEADS = 20
A_KV_GROUPS = 4
A_HPG = A_HEADS // A_KV_GROUPS
CMP_LEN = 32
CMP_STRIDE = 16
CMP_HIDDEN = 4 * HEAD_DIM
SLC_LEN = 64
SLC_TOPK = 16
WIN_LEN = 512

B_DILATIONS = ((128, 1), (512, 4), (2048, 16))
B_HPG = 4
B_HEADS = B_HPG * len(B_DILATIONS)

C_HEADS = 16
C_VDIM = 2 * HEAD_DIM

N_EXPERTS = 32
TOP_K = 4
D_EXPERT = 384
SWIGLU_LIMIT = 7.0
SWIGLU_ALPHA = 1.702

DEEPNORM_ALPHA = (2.0 * DEPTH) ** 0.25

A_Q = A_HEADS * HEAD_DIM
A_KV = A_KV_GROUPS * HEAD_DIM
A_GATE = A_HEADS * 3
B_W = B_HEADS * HEAD_DIM
C_QK = 2 * C_HEADS * HEAD_DIM
C_V = C_HEADS * C_VDIM

LANE = 128
VMEM_LIMIT = 56 * 1024 * 1024

GATE_PAD = 512
EVEN_COLS = A_Q + 6 * A_KV + 3 * B_W + GATE_PAD
CB_QA = 0
CB_KC = A_Q // LANE
CB_VC = CB_KC + A_KV // LANE
CB_KS = CB_VC + A_KV // LANE
CB_VS = CB_KS + A_KV // LANE
CB_KW = CB_VS + A_KV // LANE
CB_VW = CB_KW + A_KV // LANE
CB_QB = CB_VW + A_KV // LANE
CB_KB = CB_QB + B_W // LANE
CB_VB = CB_KB + B_W // LANE
CB_GATE = CB_VB + B_W // LANE

MM_TN = 512


def _cparams(sem):
    return pltpu.CompilerParams(dimension_semantics=sem, vmem_limit_bytes=VMEM_LIMIT)


def _nt_dot(a, b):
    return lax.dot_general(a, b, (((1,), (1,)), ((), ())), preferred_element_type=F32)


def _rope_lanes(x, tab):
    c = tab[:, 0:LANE]
    sa = tab[:, LANE:2 * LANE]
    sb = tab[:, 2 * LANE:3 * LANE]
    return (x * c + pltpu.roll(x, LANE - ROT_HALF, axis=1) * sa
            + pltpu.roll(x, ROT_HALF, axis=1) * sb)


def _mm_kernel(*refs, has_bias, act, has_rope, n_sub):
    if has_rope:
        modes_ref, a_ref, b_ref = refs[0], refs[1], refs[2]
        rest = refs[3:]
    else:
        a_ref, b_ref = refs[0], refs[1]
        rest = refs[2:]
    idx = 0
    if has_bias:
        bias_ref = rest[idx]
        idx += 1
    if has_rope:
        tab_ref = rest[idx]
        idx += 1
    o_ref = rest[idx]

    acc = jnp.dot(a_ref[...], b_ref[...], preferred_element_type=F32)
    if has_bias:
        acc = acc + bias_ref[...]
    if act == "gelu":
        acc = jax.nn.gelu(acc, approximate=True)
    if not has_rope:
        o_ref[...] = acc.astype(o_ref.dtype)
        return
    mode = modes_ref[pl.program_id(1)]

    @pl.when(mode == 0)
    def _():
        o_ref[...] = acc.astype(o_ref.dtype)

    @pl.when(mode != 0)
    def _():
        tab = tab_ref[...]
        for s in range(n_sub):
            sl = slice(s * LANE, (s + 1) * LANE)
            o_ref[:, sl] = _rope_lanes(acc[:, sl], tab).astype(o_ref.dtype)


def _mm(a, b, *, bias=None, act=None, rope_tabs=None, rope_modes=None,
        out_dtype=None, tm=1024, tn=MM_TN, name="mm"):
    out_dtype = BF16 if out_dtype is None else out_dtype
    m, k = a.shape
    k2, n = b.shape
    assert k == k2
    tm = min(tm, m)
    tn = min(tn, n)
    assert m % tm == 0 and n % tn == 0
    has_bias = bias is not None
    has_rope = rope_tabs is not None
    grid = (m // tm, n // tn)
    kern = functools.partial(_mm_kernel, has_bias=has_bias, act=act, has_rope=has_rope,
                             n_sub=tn // LANE)
    if has_rope:
        in_specs = [pl.BlockSpec((tm, k), lambda i, j, md: (i, 0)),
                    pl.BlockSpec((k, tn), lambda i, j, md: (0, j))]
        args = [a, b]
        if has_bias:
            in_specs.append(pl.BlockSpec((1, tn), lambda i, j, md: (0, j)))
            args.append(bias.reshape(1, n).astype(F32))
        in_specs.append(pl.BlockSpec((None, tm, 3 * LANE),
                                     lambda i, j, md: (jnp.maximum(md[j] - 1, 0), i, 0)))
        args.append(rope_tabs)
        gs = pltpu.PrefetchScalarGridSpec(
            num_scalar_prefetch=1, grid=grid, in_specs=in_specs,
            out_specs=pl.BlockSpec((tm, tn), lambda i, j, md: (i, j)))
        return pl.pallas_call(kern, grid_spec=gs,
                              out_shape=jax.ShapeDtypeStruct((m, n), out_dtype),
                              compiler_params=_cparams(("parallel", "arbitrary")),
                              name=name)(rope_modes, *args)
    in_specs = [pl.BlockSpec((tm, k), lambda i, j: (i, 0)),
                pl.BlockSpec((k, tn), lambda i, j: (0, j))]
    args = [a, b]
    if has_bias:
        in_specs.append(pl.BlockSpec((1, tn), lambda i, j: (0, j)))
        args.append(bias.reshape(1, n).astype(F32))
    return pl.pallas_call(kern, grid=grid, in_specs=in_specs,
                          out_specs=pl.BlockSpec((tm, tn), lambda i, j: (i, j)),
                          out_shape=jax.ShapeDtypeStruct((m, n), out_dtype),
                          compiler_params=_cparams(("parallel", "arbitrary")),
                          name=name)(*args)


def _ln_kernel(x_ref, y_ref, g_ref, b_ref, of_ref, ob_ref):
    z = DEEPNORM_ALPHA * x_ref[...] + y_ref[...].astype(F32)
    mu = jnp.mean(z, axis=-1, keepdims=True)
    zc = z - mu
    var = jnp.mean(zc * zc, axis=-1, keepdims=True)
    out = zc * lax.rsqrt(var + 1e-5) * g_ref[...] + b_ref[...]
    of_ref[...] = out
    ob_ref[...] = out.astype(BF16)


def _ln_res(x, y, g, b, tm=256):
    t, d = x.shape
    tm = min(tm, t)
    row = pl.BlockSpec((tm, d), lambda i: (i, 0))
    vec = pl.BlockSpec((1, d), lambda i: (0, 0))
    return pl.pallas_call(
        _ln_kernel, grid=(t // tm,), in_specs=[row, row, vec, vec], out_specs=[row, row],
        out_shape=[jax.ShapeDtypeStruct((t, d), F32), jax.ShapeDtypeStruct((t, d), BF16)],
        compiler_params=_cparams(("parallel",)), name="ln_res",
    )(x, y, g.reshape(1, d), b.reshape(1, d))


def _rope_tables(pos, scale):
    inv = ROPE_THETA ** (-jnp.arange(ROT_HALF, dtype=F32) / ROT_HALF)
    ang = pos.astype(F32)[:, None] * inv[None, :]
    cos = jnp.cos(ang)
    sin = jnp.sin(ang)
    n = pos.shape[0]
    ones = jnp.ones((n, HEAD_DIM - ROT_DIM), F32)
    zer = jnp.zeros((n, HEAD_DIM - ROT_HALF), F32)
    c = jnp.concatenate([cos, cos, ones], axis=1)
    sa = jnp.concatenate([-sin, zer], axis=1)
    sb = jnp.concatenate([jnp.zeros((n, ROT_HALF), F32), sin,
                          jnp.zeros((n, HEAD_DIM - ROT_DIM), F32)], axis=1)
    return jnp.concatenate([c, sa, sb], axis=1) * scale


def _swa_kernel(q_ref, kp_ref, kc_ref, vp_ref, vc_ref, *out_refs, nh, shared_kv, max_dist,
                tq, with_lse, log2_scores):
    assert not (with_lse and log2_scores)
    o_ref = out_refs[0]
    i = pl.program_id(1)
    row = lax.broadcasted_iota(jnp.int32, (tq, 2 * tq), 0)
    col = lax.broadcasted_iota(jnp.int32, (tq, 2 * tq), 1)
    dist = row + tq - col
    mask = (dist >= 0) & (dist <= max_dist) & (col + (i - 1) * tq >= 0)
    for h in range(nh):
        kv = 0 if shared_kv else h
        hs = slice(h * LANE, (h + 1) * LANE)
        ks = slice(kv * LANE, (kv + 1) * LANE)
        q = q_ref[:, hs]
        k = jnp.concatenate([kp_ref[:, ks], kc_ref[:, ks]], axis=0)
        v = jnp.concatenate([vp_ref[:, ks], vc_ref[:, ks]], axis=0)
        s = jnp.where(mask, _nt_dot(q, k), NEG_INF)
        m = jnp.max(s, axis=-1, keepdims=True)
        e = jnp.exp2(s - m) if log2_scores else jnp.exp(s - m)
        den = jnp.sum(e, axis=-1, keepdims=True)
        o = jnp.dot(e.astype(BF16), v, preferred_element_type=F32) / den
        o_ref[:, hs] = o.astype(o_ref.dtype)
        if with_lse:
            out_refs[1][:, hs] = jnp.broadcast_to(m + jnp.log(den), (tq, LANE))


def _swa(qsrc, ksrc, vsrc, *, n_r, n_tiles, tq, nh, shared_kv, max_dist, q_map, k_map, v_map,
         out_cols, o_map, with_lse, log2_scores, name):
    kvw = LANE if shared_kv else nh * LANE
    qw = nh * LANE
    prev = lambda f: (lambda r, i: (jnp.maximum(i - 1, 0), f(r)))
    cur = lambda f: (lambda r, i: (i, f(r)))
    in_specs = [pl.BlockSpec((tq, qw), cur(q_map)),
                pl.BlockSpec((tq, kvw), prev(k_map)), pl.BlockSpec((tq, kvw), cur(k_map)),
                pl.BlockSpec((tq, kvw), prev(v_map)), pl.BlockSpec((tq, kvw), cur(v_map))]
    rows = n_tiles * tq
    out_shape = [jax.ShapeDtypeStruct((rows, out_cols), BF16)]
    out_specs = [pl.BlockSpec((tq, qw), cur(o_map))]
    if with_lse:
        out_shape.append(jax.ShapeDtypeStruct((rows, out_cols), F32))
        out_specs.append(pl.BlockSpec((tq, qw), cur(o_map)))
    kern = functools.partial(_swa_kernel, nh=nh, shared_kv=shared_kv, max_dist=max_dist, tq=tq,
                             with_lse=with_lse, log2_scores=log2_scores)
    return pl.pallas_call(kern, grid=(n_r, n_tiles), in_specs=in_specs, out_specs=out_specs,
                          out_shape=out_shape,
                          compiler_params=_cparams(("parallel", "arbitrary")),
                          name=name)(qsrc, ksrc, ksrc, vsrc, vsrc)


def _band_kernel(q_ref, k_ref, v_ref, o_ref, lse_ref, *, tq, window, dil):
    i = pl.program_id(1)
    q = q_ref[...]
    s0 = i * tq
    qpos = s0 + lax.broadcasted_iota(jnp.int32, (tq, 1), 0)
    kcol = lax.broadcasted_iota(jnp.int32, (1, tq), 1)
    first = jnp.maximum(s0 - window, 0) // tq

    def body(t, carry):
        m, l, acc = carry
        start = pl.multiple_of(t * tq, tq)
        k = k_ref[pl.ds(start, tq), :]
        v = v_ref[pl.ds(start, tq), :]
        dist = qpos - (kcol + t * tq)
        valid = (dist >= 0) & (dist <= window) & ((dist & (dil - 1)) == 0)
        s = jnp.where(valid, _nt_dot(q, k), NEG_INF)
        m_new = jnp.maximum(m, jnp.max(s, axis=-1, keepdims=True))
        alpha = jnp.exp(m - m_new)
        p = jnp.where(valid, jnp.exp(s - m_new), 0.0)
        l = alpha * l + jnp.sum(p, axis=-1, keepdims=True)
        acc = alpha * acc + jnp.dot(p.astype(BF16), v, preferred_element_type=F32)
        return m_new, l, acc

    init = (jnp.full((tq, 1), NEG_INF, F32), jnp.zeros((tq, 1), F32), jnp.zeros((tq, LANE), F32))
    m, l, acc = lax.fori_loop(first, i + 1, body, init)
    o_ref[...] = (acc / l).astype(o_ref.dtype)
    lse_ref[...] = jnp.broadcast_to(m + jnp.log(l), (tq, LANE))


def _band_attn(h, t, *, q_cb, k_cb, v_cb, nh, window, dil, tq=256):
    tq = min(tq, t)
    assert dil & (dil - 1) == 0
    kern = functools.partial(_band_kernel, tq=tq, window=window, dil=dil)
    blk = pl.BlockSpec((tq, LANE), lambda hd, i: (i, hd))
    return pl.pallas_call(
        kern, grid=(nh, t // tq),
        in_specs=[pl.BlockSpec((tq, LANE), lambda hd, i: (i, q_cb + hd)),
                  pl.BlockSpec((t, LANE), lambda hd, i: (0, k_cb + hd)),
                  pl.BlockSpec((t, LANE), lambda hd, i: (0, v_cb + hd))],
        out_specs=[blk, blk],
        out_shape=[jax.ShapeDtypeStruct((t, nh * LANE), BF16), jax.ShapeDtypeStruct((t, nh * LANE), F32)],
        compiler_params=_cparams(("parallel", "arbitrary")), name="dilated_band_%d" % dil,
    )(h, h, h)


def _cmp_kernel(q_ref, kc_ref, vc_ref, ov_ref, oc_ref, sel_ref, *, tq, n_cmp_pad, n_slc):
    i = pl.program_id(1)
    tqv = i * tq + lax.broadcasted_iota(jnp.int32, (tq, 1), 0)
    cmp_end = CMP_STRIDE * lax.broadcasted_iota(jnp.int32, (1, n_cmp_pad), 1) + (CMP_LEN - 1)
    maskc = cmp_end <= tqv
    kc = kc_ref[...]
    vc = vc_ref[...]
    psum = jnp.zeros((tq, n_cmp_pad), F32)
    for h in range(A_HPG):
        hs = slice(h * LANE, (h + 1) * LANE)
        s = jnp.where(maskc, _nt_dot(q_ref[:, hs], kc), NEG_INF)
        m = jnp.max(s, axis=-1, keepdims=True)
        e = jnp.where(maskc, jnp.exp2(s - m), 0.0)
        den = jnp.sum(e, axis=-1, keepdims=True)
        p = e / jnp.maximum(den, 1e-30)
        oc_ref[:, hs] = jnp.dot(p.astype(BF16), vc, preferred_element_type=F32).astype(oc_ref.dtype)
        psum = psum + p
    ov = ov_ref[...]
    p1 = psum.astype(BF16)
    r1 = psum - p1.astype(F32)
    p2 = r1.astype(BF16)
    p3 = (r1 - p2.astype(F32)).astype(BF16)
    imp = (jnp.dot(p1, ov, preferred_element_type=F32) + jnp.dot(p2, ov, preferred_element_type=F32)
           + jnp.dot(p3, ov, preferred_element_type=F32))
    j = lax.broadcasted_iota(jnp.int32, (1, n_slc), 1)
    jf = j.astype(F32)
    tb = tqv // SLC_LEN
    forced = (j == 0) | (j == tb) | (j == tb - 1)
    val = jnp.where(forced, SEL_FORCE, jnp.where(j <= tb, imp, -SEL_FORCE))
    sel = jnp.zeros((tq, n_slc), F32)
    for _ in range(min(SLC_TOPK, n_slc)):
        m = jnp.max(val, axis=-1, keepdims=True)
        first = jnp.min(jnp.where(val == m, jf, float(n_slc)), axis=-1, keepdims=True)
        hit = jf == first
        sel = jnp.where(hit & (m > -0.5 * SEL_FORCE), 1.0, sel)
        val = jnp.where(hit, -3e38, val)
    sel_ref[...] = sel.astype(sel_ref.dtype)


def _cmp_attn(h, kc, vc, t, tq=512):
    n_cmp_pad = kc.shape[0] // A_KV_GROUPS
    n_slc = t // SLC_LEN
    ci = np.arange(n_cmp_pad)[:, None] * CMP_STRIDE
    sj = np.arange(n_slc)[None, :] * SLC_LEN
    overlap = ((ci < sj + SLC_LEN) & (ci + CMP_LEN > sj) & (np.arange(n_cmp_pad)[:, None] < t // CMP_STRIDE - 1))
    overlap = jnp.asarray(overlap.astype(np.float32), BF16)
    kern = functools.partial(_cmp_kernel, tq=tq, n_cmp_pad=n_cmp_pad, n_slc=n_slc)
    qw = A_HPG * LANE
    return pl.pallas_call(
        kern, grid=(A_KV_GROUPS, t // tq),
        in_specs=[pl.BlockSpec((tq, qw), lambda g, i: (i, g)),
                  pl.BlockSpec((n_cmp_pad, LANE), lambda g, i: (g, 0)),
                  pl.BlockSpec((n_cmp_pad, LANE), lambda g, i: (g, 0)),
                  pl.BlockSpec((n_cmp_pad, n_slc), lambda g, i: (0, 0))],
        out_specs=[pl.BlockSpec((tq, qw), lambda g, i: (i, g)),
                   pl.BlockSpec((tq, n_slc), lambda g, i: (i, g))],
        out_shape=[jax.ShapeDtypeStruct((t, A_Q), BF16),
                   jax.ShapeDtypeStruct((t, A_KV_GROUPS * n_slc), BF16)],
        compiler_params=_cparams(("parallel", "arbitrary")), name="cmp_attn_topk",
    )(h, kc, vc, overlap)


ROW_CHUNK = 32


def _flash_pipeline(n, score_stage, softmax_stage, value_stage):
    @pl.when(n > 0)
    def _():
        score_stage(0, 0)

    def step(t, slot):
        softmax_stage(slot)
        value_stage(1 - slot, jnp.maximum(t - 1, 0))
        score_stage(1 - slot, jnp.minimum(t + 1, n - 1))

    def pair(u, c):
        step(2 * u, 0)
        step(2 * u + 1, 1)
        return c

    lax.fori_loop(0, n // 2, pair, 0)
    last = jnp.maximum(n - 1, 0)

    @pl.when(n % 2 == 1)
    def _():
        step(n - 1, 0)
        value_stage(0, last)

    @pl.when(n % 2 == 0)
    def _():
        value_stage(1, last)


def _sel_kernel(q_ref, k_ref, v_ref, sel_ref, o_ref, q5_sc, s_sc, bias_sc, p_sc, acc_sc, m_sc, l_sc,
                alpha_sc, *, tq, tk, n_slc):
    i = pl.program_id(1)
    nh = A_HPG
    for h in range(nh):
        q5_sc[h * tq:(h + 1) * tq, :] = q_ref[:, h * LANE:(h + 1) * LANE]
    m_sc[...] = jnp.full(m_sc.shape, NEG_INF, F32)
    l_sc[...] = jnp.zeros_like(l_sc)
    acc_sc[...] = jnp.zeros_like(acc_sc)
    p_sc[1] = jnp.zeros(p_sc.shape[1:], p_sc.dtype)
    alpha_sc[1] = jnp.ones(alpha_sc.shape[1:], F32)
    selb = sel_ref[...]
    tqv = i * tq + lax.broadcasted_iota(jnp.int32, (tq, 1), 0)
    blk_per_tile = tk // SLC_LEN
    blk_gap = (lax.broadcasted_iota(jnp.int32, (n_slc, tk), 0)
               - lax.broadcasted_iota(jnp.int32, (n_slc, tk), 1) // SLC_LEN)
    kcol = lax.broadcasted_iota(jnp.int32, (1, tk), 1)
    n_kv = (i * tq + tq + tk - 1) // tk
    nrep = tk // LANE

    def score_stage(slot, t):
        k = k_ref[pl.ds(pl.multiple_of(t * tk, tk), tk), :]
        s_sc[slot] = _nt_dot(q5_sc[...], k)
        expand = jnp.where(blk_gap == t * blk_per_tile, 1.0, 0.0).astype(BF16)
        picked = jnp.dot(selb, expand, preferred_element_type=F32)
        bias_sc[slot] = jnp.where((picked > 0.5) & (kcol + t * tk <= tqv), 0.0, NEG_INF)

    def softmax_stage(slot):
        for c in range(nh * tq // ROW_CHUNK):
            rows = slice(c * ROW_CHUNK, (c + 1) * ROW_CHUNK)
            r0 = (c * ROW_CHUNK) % tq
            s = s_sc[slot, rows, :] + bias_sc[slot, r0:r0 + ROW_CHUNK, :]
            m_old = m_sc[rows, :]
            m_new = jnp.maximum(m_old, jnp.max(s, axis=-1, keepdims=True))
            alpha = jnp.exp2(m_old - m_new)
            p = jnp.exp2(s - jnp.concatenate([m_new] * nrep, axis=1))
            l_sc[rows, :] = alpha * l_sc[rows, :] + jnp.sum(p, axis=-1, keepdims=True)
            m_sc[rows, :] = m_new
            alpha_sc[slot, rows, :] = alpha
            p_sc[slot, rows, :] = p.astype(p_sc.dtype)

    def value_stage(slot, t):
        v = v_ref[pl.ds(pl.multiple_of(t * tk, tk), tk), :]
        acc_sc[...] = acc_sc[...] * alpha_sc[slot] + jnp.dot(p_sc[slot], v, preferred_element_type=F32)

    _flash_pipeline(n_kv, score_stage, softmax_stage, value_stage)
    o = acc_sc[...] / l_sc[...]
    for h in range(nh):
        o_ref[:, h * LANE:(h + 1) * LANE] = o[h * tq:(h + 1) * tq].astype(o_ref.dtype)


def _sel_attn(h, sel, t, tq=128, tk=512):
    n_slc = t // SLC_LEN
    tk = min(tk, t)
    qw = A_HPG * LANE
    rows = A_HPG * tq
    kern = functools.partial(_sel_kernel, tq=tq, tk=tk, n_slc=n_slc)
    return pl.pallas_call(
        kern, grid=(A_KV_GROUPS, t // tq),
        in_specs=[pl.BlockSpec((tq, qw), lambda g, i: (i, g)),
                  pl.BlockSpec((t, LANE), lambda g, i: (0, CB_KS + g)),
                  pl.BlockSpec((t, LANE), lambda g, i: (0, CB_VS + g)),
                  pl.BlockSpec((tq, n_slc), lambda g, i: (i, g))],
        out_specs=pl.BlockSpec((tq, qw), lambda g, i: (i, g)),
        out_shape=jax.ShapeDtypeStruct((t, A_Q), BF16),
        scratch_shapes=[pltpu.VMEM((rows, LANE), BF16), pltpu.VMEM((2, rows, tk), F32),
                        pltpu.VMEM((2, tq, tk), F32), pltpu.VMEM((2, rows, tk), BF16),
                        pltpu.VMEM((rows, LANE), F32), pltpu.VMEM((rows, LANE), F32),
                        pltpu.VMEM((rows, LANE), F32), pltpu.VMEM((2, rows, LANE), F32)],
        compiler_params=_cparams(("parallel", "arbitrary")), name="sel_attn",
    )(h, h, h, sel)


def _mixprep_kernel(oc_ref, os_ref, ow_ref, gate_ref, ob_ref, lse_ref, out_ref):
    gate = jax.nn.sigmoid(gate_ref[...].astype(F32))
    for h in range(A_HEADS):
        hs = slice(h * LANE, (h + 1) * LANE)
        o = (gate[:, 3 * h:3 * h + 1] * oc_ref[:, hs].astype(F32)
             + gate[:, 3 * h + 1:3 * h + 2] * os_ref[:, hs].astype(F32)
             + gate[:, 3 * h + 2:3 * h + 3] * ow_ref[:, hs].astype(F32))
        out_ref[:, hs] = o.astype(out_ref.dtype)
    ng = len(B_DILATIONS)
    for hi in range(B_HPG):
        lses = [lse_ref[:, (g * B_HPG + hi) * LANE:(g * B_HPG + hi + 1) * LANE] for g in range(ng)]
        m = functools.reduce(jnp.maximum, lses)
        es = [jnp.exp(x - m) for x in lses]
        den = functools.reduce(lambda a, b: a + b, es)
        for g in range(ng):
            src = slice((g * B_HPG + hi) * LANE, (g * B_HPG + hi + 1) * LANE)
            dst = slice(A_Q + (g * B_HPG + hi) * LANE, A_Q + (g * B_HPG + hi + 1) * LANE)
            out_ref[:, dst] = (ob_ref[:, src].astype(F32) * (es[g] / den)).astype(out_ref.dtype)


def _mixprep(o_c, o_s, o_w, h, o_b, lse_b, t, tq=256):
    tq = min(tq, t)
    a = pl.BlockSpec((tq, A_Q), lambda i: (i, 0))
    b = pl.BlockSpec((tq, B_W), lambda i: (i, 0))
    return pl.pallas_call(
        _mixprep_kernel, grid=(t // tq,),
        in_specs=[a, a, a, pl.BlockSpec((tq, LANE), lambda i: (i, CB_GATE)), b, b],
        out_specs=pl.BlockSpec((tq, A_Q + B_W), lambda i: (i, 0)),
        out_shape=jax.ShapeDtypeStruct((t, A_Q + B_W), BF16),
        compiler_params=_cparams(("parallel",)), name="even_mixprep",
    )(o_c, o_s, o_w, h, o_b, lse_b)


def _diff_kernel(q_ref, k_ref, v_ref, lq1_ref, lk1_ref, lq2_ref, lk2_ref, g_ref, o_ref,
                 s_sc, p_sc, acc_sc, m_sc, l_sc, alpha_sc, *, tq, tk, lam_init):
    i = pl.program_id(1)
    q1 = q_ref[:, 0:LANE]
    q2 = q_ref[:, LANE:2 * LANE]
    lam = (jnp.exp(jnp.sum(lq1_ref[...] * lk1_ref[...], axis=-1, keepdims=True))
           - jnp.exp(jnp.sum(lq2_ref[...] * lk2_ref[...], axis=-1, keepdims=True)) + lam_init)

    m_sc[...] = jnp.full(m_sc.shape, NEG_INF, F32)
    l_sc[...] = jnp.zeros_like(l_sc)
    acc_sc[...] = jnp.zeros_like(acc_sc)
    p_sc[1] = jnp.zeros(p_sc.shape[1:], p_sc.dtype)
    alpha_sc[1] = jnp.ones(alpha_sc.shape[1:], F32)
    nrep = tk // LANE
    rep = C_VDIM // LANE

    def score_stage(slot, start):
        k = k_ref[pl.ds(start, tk), :]
        s_sc[slot, 0] = _nt_dot(q1, k[:, 0:LANE])
        s_sc[slot, 1] = _nt_dot(q2, k[:, LANE:2 * LANE])

    def softmax_stage(slot, diag_off):
        for hd in range(2):
            for c in range(tq // ROW_CHUNK):
                rows = slice(c * ROW_CHUNK, (c + 1) * ROW_CHUNK)
                s = s_sc[slot, hd, rows, :]
                if diag_off is not None:
                    row = lax.broadcasted_iota(jnp.int32, (ROW_CHUNK, tk), 0) + c * ROW_CHUNK
                    col = lax.broadcasted_iota(jnp.int32, (ROW_CHUNK, tk), 1) + diag_off
                    s = jnp.where(col <= row, s, NEG_INF)
                m_old = m_sc[hd, rows, :]
                m_new = jnp.maximum(m_old, jnp.max(s, axis=-1, keepdims=True))
                alpha = jnp.exp2(m_old - m_new)
                p = jnp.exp2(s - jnp.concatenate([m_new] * nrep, axis=1))
                l_sc[hd, rows, :] = alpha * l_sc[hd, rows, :] + jnp.sum(p, axis=-1, keepdims=True)
                m_sc[hd, rows, :] = m_new
                alpha_sc[slot, hd, rows, :] = alpha
                p_sc[slot, hd, rows, :] = p.astype(p_sc.dtype)

    def value_stage(slot, start):
        v = v_ref[pl.ds(start, tk), :]
        for hd in range(2):
            pv = jnp.dot(p_sc[slot, hd], v, preferred_element_type=F32)
            alpha = alpha_sc[slot, hd]
            acc_sc[hd] = acc_sc[hd] * jnp.concatenate([alpha] * rep, axis=1) + pv

    per = tq // tk
    n_full = i * per
    _flash_pipeline(n_full, lambda slot, t: score_stage(slot, pl.multiple_of(t * tk, tk)),
                    lambda slot: softmax_stage(slot, None),
                    lambda slot, t: value_stage(slot, pl.multiple_of(t * tk, tk)))
    for u in range(per):
        start = pl.multiple_of(i * tq + u * tk, tk)
        score_stage(0, start)
        softmax_stage(0, u * tk)
        value_stage(0, start)
    l1 = jnp.concatenate([l_sc[0]] * rep, axis=1)
    l2 = jnp.concatenate([l_sc[1]] * rep, axis=1)
    o = acc_sc[0] / l1 - lam * (acc_sc[1] / l2)
    o = o * lax.rsqrt(jnp.mean(o * o, axis=-1, keepdims=True) + 1e-5) * g_ref[...] * (1.0 - lam_init)
    o_ref[...] = o.astype(o_ref.dtype)


def _diff_attn(h, lq1, lk1, lq2, lk2, sub_g, layer, t, tq=512, tk=512):
    tq = min(tq, t)
    tk = min(tk, tq)
    lam_init = 0.8 - 0.6 * math.exp(-0.3 * layer)
    kern = functools.partial(_diff_kernel, tq=tq, tk=tk, lam_init=lam_init)
    vec = pl.BlockSpec((1, LANE), lambda hh, i: (0, 0))
    qkb = C_QK // C_VDIM
    return pl.pallas_call(
        kern, grid=(C_HEADS, t // tq),
        in_specs=[pl.BlockSpec((tq, C_VDIM), lambda hh, i: (i, hh)),
                  pl.BlockSpec((t, C_VDIM), lambda hh, i: (0, qkb + hh)),
                  pl.BlockSpec((t, C_VDIM), lambda hh, i: (0, 2 * qkb + hh)),
                  vec, vec, vec, vec, pl.BlockSpec((1, C_VDIM), lambda hh, i: (0, 0))],
        out_specs=pl.BlockSpec((tq, C_VDIM), lambda hh, i: (i, hh)),
        out_shape=jax.ShapeDtypeStruct((t, C_V), BF16),
        scratch_shapes=[pltpu.VMEM((2, 2, tq, tk), F32), pltpu.VMEM((2, 2, tq, tk), BF16),
                        pltpu.VMEM((2, tq, C_VDIM), F32), pltpu.VMEM((2, tq, LANE), F32),
                        pltpu.VMEM((2, tq, LANE), F32), pltpu.VMEM((2, 2, tq, LANE), F32)],
        compiler_params=_cparams(("parallel", "arbitrary")), name="diff_attn",
    )(h, h, h, lq1.reshape(1, LANE).astype(F32), lk1.reshape(1, LANE).astype(F32),
      lq2.reshape(1, LANE).astype(F32), lk2.reshape(1, LANE).astype(F32),
      sub_g.reshape(1, C_VDIM).astype(F32))


ROUTE_EID = 0
ROUTE_RANK = 4
ROUTE_W = 8
MOE_TM = 256


def _router_kernel(x_ref, w_ref, b_ref, comb_ref, route_ref, cnt_ref, carry_ref):
    i = pl.program_id(0)

    @pl.when(i == 0)
    def _():
        carry_ref[...] = jnp.zeros_like(carry_ref)

    logits = jnp.dot(x_ref[...], w_ref[...], preferred_element_type=F32,
                     precision=lax.Precision.HIGHEST) + b_ref[...]
    tm, ne = logits.shape
    lane = lax.broadcasted_iota(jnp.int32, (1, ne), 1)
    jf = lane.astype(F32)
    val = logits
    tops, hits, firsts = [], [], []
    for _ in range(TOP_K):
        m = jnp.max(val, axis=-1, keepdims=True)
        first = jnp.min(jnp.where(val == m, jf, float(ne)), axis=-1, keepdims=True)
        hit = jf == first
        tops.append(m)
        hits.append(hit)
        firsts.append(first)
        val = jnp.where(hit, -3e38, val)
    es = [jnp.exp(x - tops[0]) for x in tops]
    den = functools.reduce(lambda a, b: a + b, es)
    comb = jnp.zeros((tm, ne), F32)
    picked = jnp.zeros((tm, ne), F32)
    for e, hit in zip(es, hits):
        comb = jnp.where(hit, e / den, comb)
        picked = jnp.where(hit, 1.0, picked)
    comb_ref[...] = comb
    row = lax.broadcasted_iota(jnp.int32, (tm, tm), 0)
    col = lax.broadcasted_iota(jnp.int32, (tm, tm), 1)
    tri = jnp.where(row > col, 1.0, 0.0).astype(BF16)
    rank = jnp.dot(tri, picked.astype(BF16), preferred_element_type=F32) + carry_ref[...]
    carry_ref[...] += jnp.sum(picked, axis=0, keepdims=True)
    cnt_ref[...] = carry_ref[...]
    route = jnp.zeros((tm, ne), F32)
    for k in range(TOP_K):
        rk = jnp.sum(jnp.where(hits[k], rank, 0.0), axis=-1, keepdims=True)
        route = jnp.where(lane == ROUTE_EID + k, firsts[k], route)
        route = jnp.where(lane == ROUTE_RANK + k, rk, route)
        route = jnp.where(lane == ROUTE_W + k, es[k] / den, route)
    route_ref[...] = route


def _router(x, rw, rb, tm=256):
    t, d = x.shape
    tm = min(tm, t)
    rw_p = jnp.zeros((d, LANE), F32).at[:, :N_EXPERTS].set(rw)
    rb_p = jnp.full((1, LANE), NEG_INF, F32).at[0, :N_EXPERTS].set(rb)
    blk = pl.BlockSpec((tm, LANE), lambda i: (i, 0))
    return pl.pallas_call(
        _router_kernel, grid=(t // tm,),
        in_specs=[pl.BlockSpec((tm, d), lambda i: (i, 0)),
                  pl.BlockSpec((d, LANE), lambda i: (0, 0)),
                  pl.BlockSpec((1, LANE), lambda i: (0, 0))],
        out_specs=[blk, blk, pl.BlockSpec((1, LANE), lambda i: (0, 0))],
        out_shape=[jax.ShapeDtypeStruct((t, LANE), F32), jax.ShapeDtypeStruct((t, LANE), F32),
                   jax.ShapeDtypeStruct((1, LANE), F32)],
        scratch_shapes=[pltpu.VMEM((1, LANE), F32)],
        compiler_params=_cparams(("arbitrary",)), name="moe_router",
    )(x, rw_p, rb_p)


def _row_copy(src, src_row, dst, dst_row, sem):
    return pltpu.make_async_copy(src.at[pl.ds(src_row, 1)], dst.at[pl.ds(dst_row, 1)], sem)


def _dispatch_kernel(gend_ref, dest_ref, x_ref, xs_hbm, zeros_ref, sem, *, tm):
    i = pl.program_id(0)

    @pl.when(i == 0)
    def _():
        zeros_ref[...] = jnp.zeros_like(zeros_ref)

        def clear(e):
            start = pl.multiple_of(gend_ref[e] - MOE_TM, MOE_TM)
            return pltpu.make_async_copy(zeros_ref, xs_hbm.at[pl.ds(start, MOE_TM)], sem)
        for e in range(N_EXPERTS):
            clear(e).start()
        for e in range(N_EXPERTS):
            clear(e).wait()

        def clear_tail(j, c):
            cp = pltpu.make_async_copy(
                zeros_ref, xs_hbm.at[pl.ds(pl.multiple_of(j * MOE_TM, MOE_TM), MOE_TM)], sem)
            cp.start()
            cp.wait()
            return c

        lax.fori_loop(gend_ref[N_EXPERTS - 1] // MOE_TM, xs_hbm.shape[0] // MOE_TM, clear_tail, 0)

    def issue(r, c):
        for k in range(TOP_K):
            _row_copy(x_ref, r, xs_hbm, dest_ref[TOP_K * r + k], sem).start()
        return c

    lax.fori_loop(0, tm, issue, 0)

    def drain(r, c):
        for k in range(TOP_K):
            _row_copy(x_ref, 0, xs_hbm, 0, sem).wait()
        return c

    lax.fori_loop(0, tm, drain, 0)


def _dispatch(xf, dest_flat, gend, n_rows, tm=256):
    t, d = xf.shape
    tm = min(tm, t)
    gs = pltpu.PrefetchScalarGridSpec(
        num_scalar_prefetch=1, grid=(t // tm,),
        in_specs=[pl.BlockSpec((TOP_K * tm,), lambda i, ge: (i,), memory_space=pltpu.SMEM),
                  pl.BlockSpec((tm, d), lambda i, ge: (i, 0))],
        out_specs=pl.BlockSpec(memory_space=pl.ANY),
        scratch_shapes=[pltpu.VMEM((MOE_TM, d), F32), pltpu.SemaphoreType.DMA(())])
    return pl.pallas_call(
        functools.partial(_dispatch_kernel, tm=tm), grid_spec=gs, out_shape=jax.ShapeDtypeStruct((n_rows, d), F32),
        compiler_params=_cparams(("arbitrary",)), name="moe_dispatch",
    )(gend, dest_flat, xf)


def _expert_kernel(eid_ref, nused_ref, xs_ref, wgu_ref, bgu_ref, wd_ref, pick_ref, ys_ref):
    j = pl.program_id(0)

    @pl.when(j < nused_ref[0])
    def _():
        x = xs_ref[...].astype(BF16)
        hgu = jnp.dot(x, wgu_ref[...], preferred_element_type=F32) + bgu_ref[...]
        glu = jnp.minimum(hgu, SWIGLU_LIMIT)
        gated = glu * jax.nn.sigmoid(SWIGLU_ALPHA * glu)
        lin = jnp.clip(hgu, -SWIGLU_LIMIT, SWIGLU_LIMIT) + 1.0
        prod = jnp.concatenate(
            [gated[:, s:s + LANE] * pltpu.roll(lin[:, s:s + LANE], LANE - 1, axis=1)
             for s in range(0, 2 * D_EXPERT, LANE)], axis=1)
        act = jnp.dot(prod.astype(BF16), pick_ref[...], preferred_element_type=F32)
        ys_ref[...] = jnp.dot(act.astype(BF16), wd_ref[...], preferred_element_type=F32)

    @pl.when(j >= nused_ref[0])
    def _():
        ys_ref[...] = jnp.zeros_like(ys_ref)


def _experts(xs, tile_eid, nused, wgu, bgu, wd):
    n_rows, d = xs.shape
    n_tiles = n_rows // MOE_TM
    pick = np.zeros((2 * D_EXPERT, D_EXPERT), np.float32)
    pick[2 * np.arange(D_EXPERT), np.arange(D_EXPERT)] = 1.0
    tile = lambda j, eid, nu: (jnp.minimum(j, nu[0] - 1), 0)
    gs = pltpu.PrefetchScalarGridSpec(
        num_scalar_prefetch=2, grid=(n_tiles,),
        in_specs=[pl.BlockSpec((MOE_TM, d), tile),
                  pl.BlockSpec((None, d, 2 * D_EXPERT), lambda j, eid, nu: (eid[j], 0, 0)),
                  pl.BlockSpec((None, 1, 2 * D_EXPERT), lambda j, eid, nu: (eid[j], 0, 0)),
                  pl.BlockSpec((None, D_EXPERT, d), lambda j, eid, nu: (eid[j], 0, 0)),
                  pl.BlockSpec((2 * D_EXPERT, D_EXPERT), lambda j, eid, nu: (0, 0))],
        out_specs=pl.BlockSpec((MOE_TM, d), lambda j, eid, nu: (j, 0)))
    return pl.pallas_call(
        _expert_kernel, grid_spec=gs, out_shape=jax.ShapeDtypeStruct((n_rows, d), F32),
        compiler_params=_cparams(("arbitrary",)), name="moe_experts",
    )(tile_eid, nused, xs, wgu, bgu, wd, jnp.asarray(pick, BF16))


def _combine_ln_kernel(dest_ref, x_ref, route_ref, comb_ref, bd_ref, g_ref, b_ref, ys_hbm,
                       of_ref, ob_ref, ybuf, sem, *, tm):
    def issue(r, c):
        for k in range(TOP_K):
            idx = TOP_K * r + k
            src_row = dest_ref[idx // LANE, idx % LANE]
            _row_copy(ys_hbm, src_row, ybuf.at[k], r, sem).start()
        return c

    lax.fori_loop(0, tm, issue, 0)
    y = jnp.dot(comb_ref[...], bd_ref[...], preferred_element_type=F32,
                precision=lax.Precision.HIGHEST)

    def drain(r, c):
        for k in range(TOP_K):
            _row_copy(ys_hbm, 0, ybuf.at[k], 0, sem).wait()
        return c

    lax.fori_loop(0, tm, drain, 0)
    route = route_ref[...]
    for k in range(TOP_K):
        y = y + route[:, ROUTE_W + k:ROUTE_W + k + 1] * ybuf[k]
    z = DEEPNORM_ALPHA * x_ref[...] + y
    mu = jnp.mean(z, axis=-1, keepdims=True)
    zc = z - mu
    var = jnp.mean(zc * zc, axis=-1, keepdims=True)
    out = zc * lax.rsqrt(var + 1e-5) * g_ref[...] + b_ref[...]
    of_ref[...] = out
    ob_ref[...] = out.astype(BF16)


def _combine_ln(xf, ys, dest_flat, route, comb, bd, g, b, tm=128):
    t, d = xf.shape
    tm = min(tm, t)
    bd_p = jnp.zeros((LANE, d), F32).at[:N_EXPERTS].set(bd)
    row = pl.BlockSpec((tm, d), lambda i: (i, 0))
    lanes = pl.BlockSpec((tm, LANE), lambda i: (i, 0))
    vec = pl.BlockSpec((1, d), lambda i: (0, 0))
    kern = functools.partial(_combine_ln_kernel, tm=tm)
    idx_rows = TOP_K * tm // LANE
    dest3 = dest_flat.reshape(t // tm, idx_rows, LANE)
    return pl.pallas_call(
        kern, grid=(t // tm,),
        in_specs=[pl.BlockSpec((None, idx_rows, LANE), lambda i: (i, 0, 0), memory_space=pltpu.SMEM),
                  row, lanes, lanes, pl.BlockSpec((LANE, d), lambda i: (0, 0)), vec, vec,
                  pl.BlockSpec(memory_space=pl.ANY)],
        out_specs=[row, row],
        out_shape=[jax.ShapeDtypeStruct((t, d), F32), jax.ShapeDtypeStruct((t, d), BF16)],
        scratch_shapes=[pltpu.VMEM((TOP_K, tm, d), F32), pltpu.SemaphoreType.DMA(())],
        compiler_params=_cparams(("arbitrary",)), name="moe_combine_ln",
    )(dest3, xf, route, comb, bd_p, g.reshape(1, d), b.reshape(1, d), ys)


def _moe_ln(xf, rw, rb, wgu, bgu, wd, bd, g, b):
    t, d = xf.shape
    comb, route, cnt = _router(xf, rw, rb)
    cnt = cnt[0, :N_EXPERTS].astype(jnp.int32)
    tiles_e = jnp.maximum((cnt + MOE_TM - 1) // MOE_TM, 1)
    gend = jnp.cumsum(tiles_e) * MOE_TM
    gstart = gend - tiles_e * MOE_TM
    eid = route[:, ROUTE_EID:ROUTE_EID + TOP_K].astype(jnp.int32)
    rank = route[:, ROUTE_RANK:ROUTE_RANK + TOP_K].astype(jnp.int32)
    onehot = eid[..., None] == jnp.arange(N_EXPERTS, dtype=jnp.int32)
    dest = jnp.sum(jnp.where(onehot, gstart, 0), axis=-1) + rank
    dest_flat = dest.reshape(-1).astype(jnp.int32)
    n_tiles = (t * TOP_K) // MOE_TM + N_EXPERTS
    tile_eid = jnp.minimum(jnp.searchsorted(gend, jnp.arange(n_tiles, dtype=jnp.int32) * MOE_TM,
                                            side="right"), N_EXPERTS - 1).astype(jnp.int32)
    nused = (gend[-1:] // MOE_TM).astype(jnp.int32)
    xs = _dispatch(xf, dest_flat, gend.astype(jnp.int32), n_tiles * MOE_TM)
    ys = _experts(xs, tile_eid, nused, wgu.astype(BF16), bgu[:, None, :].astype(F32), wd.astype(BF16))
    return _combine_ln(xf, ys, dest_flat, route, comb, bd.astype(F32), g, b)


def _compress(kv, pe, w1, b1, w2, b2, t, rope_tab):
    g = A_KV_GROUPS
    nch = t // CMP_STRIDE
    n_pad = nch
    chunks = kv.reshape(nch, CMP_STRIDE, g, HEAD_DIM)
    blocks = jnp.concatenate([chunks[:-1], chunks[1:]], axis=1)
    flat = blocks.transpose(2, 0, 1, 3).reshape(g, nch - 1, CMP_LEN * HEAD_DIM)
    flat = jnp.pad(flat, ((0, 0), (0, 1), (0, 0))).reshape(g * n_pad, CMP_LEN * HEAD_DIM)
    w1b = w1.astype(BF16)
    pe_rows = jnp.zeros((8, CMP_LEN * HEAD_DIM), F32).at[0].set(pe.reshape(-1)).astype(BF16)
    pe_term = _mm(pe_rows, w1b, out_dtype=F32, name="cmp_pe")[0]
    hid = _mm(flat, w1b, bias=pe_term + b1, act="gelu", name="cmp_mlp1")
    if rope_tab is None:
        return _mm(hid, w2.astype(BF16), bias=b2, tn=LANE, name="cmp_mlp2")
    return _mm(hid, w2.astype(BF16), bias=b2, tn=LANE, rope_tabs=rope_tab[None],
               rope_modes=jnp.ones((1,), jnp.int32), name="cmp_mlp2_rope")


def _even_w_kernel(a_ref, b_ref, o_ref, *, first_shifted, gate_tile):
    j = pl.program_id(1)
    a = a_ref[...]
    tn = a.shape[1]

    @pl.when(j < first_shifted)
    def _():
        o_ref[...] = a.astype(o_ref.dtype)

    @pl.when((j >= first_shifted) & (j < gate_tile))
    def _():
        b = b_ref[...]
        o_ref[...] = jnp.concatenate([a[:, A_GATE:], b[:, :A_GATE]], axis=1).astype(o_ref.dtype)

    @pl.when(j == gate_tile)
    def _():
        lane = lax.broadcasted_iota(jnp.int32, a.shape, 1)
        o_ref[...] = jnp.where(lane < A_GATE, a, 0.0).astype(o_ref.dtype)


def _even_w_layout(w_in, tr=512):
    d = w_in.shape[0]
    tr = min(tr, d)
    first_shifted = (A_Q + 6 * A_KV) // MM_TN
    gate_tile = EVEN_COLS // MM_TN - 1
    last_in = (w_in.shape[1] - 1) // MM_TN
    kern = functools.partial(_even_w_kernel, first_shifted=first_shifted, gate_tile=gate_tile)
    return pl.pallas_call(
        kern, grid=(d // tr, EVEN_COLS // MM_TN),
        in_specs=[pl.BlockSpec((tr, MM_TN), lambda i, j: (i, jnp.where(j == gate_tile, first_shifted, j))),
                  pl.BlockSpec((tr, MM_TN), lambda i, j: (i, jnp.minimum(j + 1, last_in)))],
        out_specs=pl.BlockSpec((tr, MM_TN), lambda i, j: (i, j)),
        out_shape=jax.ShapeDtypeStruct((d, EVEN_COLS), BF16),
        compiler_params=_cparams(("parallel", "arbitrary")), name="even_w_layout",
    )(w_in, w_in)


def _even_mixer(xb, w_in, w_out, cmpk, cmpv, tabs):
    t, d = xb.shape
    w = _even_w_layout(w_in)
    tile_modes = np.zeros((EVEN_COLS // MM_TN,), np.int32)
    per = MM_TN // LANE
    for cb, nblk, mode in ((CB_QA, A_Q // LANE, 3), (CB_KS, A_KV // LANE, 1), (CB_KW, A_KV // LANE, 1),
                           (CB_QB, B_W // LANE, 2), (CB_KB, B_W // LANE, 1)):
        tile_modes[cb // per:(cb + nblk) // per] = mode
    h = _mm(xb, w, rope_tabs=tabs["qk"], rope_modes=jnp.asarray(tile_modes), name="even_in_proj")

    n_pad = t // CMP_STRIDE
    kc = _compress(h[:, CB_KC * LANE:CB_VC * LANE], *cmpk, t, jnp.tile(tabs["cmp"], (A_KV_GROUPS, 1)))
    vc = _compress(h[:, CB_VC * LANE:CB_KS * LANE], *cmpv, t, None)
    o_c, sel = _cmp_attn(h, kc, vc, t)
    o_s = _sel_attn(h, sel, t)

    tqw = min(WIN_LEN, t)
    (o_w,) = _swa(h, h, h, n_r=A_KV_GROUPS, n_tiles=t // tqw, tq=tqw, nh=A_HPG, shared_kv=True,
                  max_dist=WIN_LEN - 1, q_map=lambda r: r, k_map=lambda r: CB_KW + r,
                  v_map=lambda r: CB_VW + r, out_cols=A_Q, o_map=lambda r: r, with_lse=False,
                  log2_scores=True,
                  name="nsa_window")

    ob_parts, lse_parts = [], []
    for gi, (window, dil) in enumerate(B_DILATIONS):
        if dil == 1:
            tqd = min(max(window, LANE), t)
            o, lse = _swa(h, h, h, n_r=1, n_tiles=t // tqd, tq=tqd, nh=B_HPG, shared_kv=False,
                          max_dist=window, q_map=lambda r, b=CB_QB // B_HPG + gi: b,
                          k_map=lambda r, b=CB_KB // B_HPG + gi: b,
                          v_map=lambda r, b=CB_VB // B_HPG + gi: b,
                          out_cols=B_HPG * LANE, o_map=lambda r: r, with_lse=True, log2_scores=False,
                          name="dilated_1")
        else:
            o, lse = _band_attn(h, t, q_cb=CB_QB + gi * B_HPG, k_cb=CB_KB + gi * B_HPG,
                                v_cb=CB_VB + gi * B_HPG, nh=B_HPG, window=window, dil=dil)
        ob_parts.append(o)
        lse_parts.append(lse)
    o_b = jnp.concatenate(ob_parts, axis=1)
    lse_b = jnp.concatenate(lse_parts, axis=1)
    mix_in = _mixprep(o_c, o_s, o_w, h, o_b, lse_b, t)
    return _mm(mix_in, w_out.astype(BF16), out_dtype=F32, name="even_out_proj")


def _odd_mixer(xb, w_in, w_out, lq1, lk1, lq2, lk2, sub_g, layer, tabs):
    t, d = xb.shape
    per = MM_TN // LANE
    tile_modes = np.zeros(((2 * C_QK + C_V) // MM_TN,), np.int32)
    tile_modes[:C_QK // MM_TN] = 3
    tile_modes[C_QK // MM_TN:2 * C_QK // MM_TN] = 1
    h = _mm(xb, w_in.astype(BF16), rope_tabs=tabs["qk"], rope_modes=jnp.asarray(tile_modes),
            name="odd_in_proj")
    o = _diff_attn(h, lq1, lk1, lq2, lk2, sub_g, layer, t)
    return _mm(o, w_out.astype(BF16), out_dtype=F32, name="odd_out_proj")


def kernel(x, even_w_in, even_w_out, cmpk_pe, cmpk_w1, cmpk_b1, cmpk_w2, cmpk_b2, cmpv_pe, cmpv_w1, cmpv_b1, cmpv_w2, cmpv_b2, odd_w_in, odd_w_out, lam_q1, lam_k1, lam_q2, lam_k2, subln_g, ln_mix_g, ln_mix_b, ln_ffn_g, ln_ffn_b, router_w, router_b, exp_w_gu, exp_b_gu, exp_w_down, exp_b_down):
    bsz, t, d = x.shape
    assert bsz == 1
    xf = x.reshape(t, d)
    xb = xf.astype(BF16)
    pos = jnp.arange(t, dtype=jnp.int32)
    cmp_end = jnp.arange(t // CMP_STRIDE, dtype=jnp.int32) * CMP_STRIDE + (CMP_LEN - 1)
    tabs = {
        "qk": jnp.stack([_rope_tables(pos, 1.0), _rope_tables(pos, HEAD_DIM ** -0.5),
                         _rope_tables(pos, HEAD_DIM ** -0.5 * math.log2(math.e))]),
        "cmp": _rope_tables(cmp_end, 1.0),
    }
    for layer in range(DEPTH):
        if layer % 2 == 0:
            e = layer // 2
            mix = _even_mixer(xb, even_w_in[e], even_w_out[e],
                              (cmpk_pe[e], cmpk_w1[e], cmpk_b1[e], cmpk_w2[e], cmpk_b2[e]),
                              (cmpv_pe[e], cmpv_w1[e], cmpv_b1[e], cmpv_w2[e], cmpv_b2[e]), tabs)
        else:
            o = layer // 2
            mix = _odd_mixer(xb, odd_w_in[o], odd_w_out[o], lam_q1[o], lam_k1[o], lam_q2[o],
                             lam_k2[o], subln_g[o], layer, tabs)
        xf, xb = _ln_res(xf, mix, ln_mix_g[layer], ln_mix_b[layer])
        xf, xb = _moe_ln(xf, router_w[layer], router_b[layer], exp_w_gu[layer], exp_b_gu[layer],
                         exp_w_down[layer], exp_b_down[layer], ln_ffn_g[layer], ln_ffn_b[layer])
    return xf.reshape(bsz, t, d)
```

```python
import functools
import math

import jax
import jax.numpy as jnp
import numpy as np
from jax import lax
from jax.experimental import pallas as pl
from jax.experimental.pallas import tpu as pltpu

F32 = jnp.float32
BF16 = jnp.bfloat16

DEPTH = 4
HEAD_DIM = 128
ROPE_THETA = 500000.0
ROT_DIM = HEAD_DIM // 4
ROT_HALF = ROT_DIM // 2
NEG_INF = -1e30
SEL_FORCE = 1e9

A_HEADS = 20
A_KV_GROUPS = 4
A_HPG = A_HEADS // A_KV_GROUPS
CMP_LEN = 32
CMP_STRIDE = 16
CMP_HIDDEN = 4 * HEAD_DIM
SLC_LEN = 64
SLC_TOPK = 16
WIN_LEN = 512

B_DILATIONS = ((128, 1), (512, 4), (2048, 16))
B_HPG = 4
B_HEADS = B_HPG * len(B_DILATIONS)

C_HEADS = 16
C_VDIM = 2 * HEAD_DIM

N_EXPERTS = 32
TOP_K = 4
D_EXPERT = 384
SWIGLU_LIMIT = 7.0
SWIGLU_ALPHA = 1.702

DEEPNORM_ALPHA = (2.0 * DEPTH) ** 0.25

A_Q = A_HEADS * HEAD_DIM
A_KV = A_KV_GROUPS * HEAD_DIM
A_GATE = A_HEADS * 3
B_W = B_HEADS * HEAD_DIM
C_QK = 2 * C_HEADS * HEAD_DIM
C_V = C_HEADS * C_VDIM

LANE = 128
VMEM_LIMIT = 56 * 1024 * 1024

GATE_PAD = 512
EVEN_COLS = A_Q + 6 * A_KV + 3 * B_W + GATE_PAD
CB_QA = 0
CB_KC = A_Q // LANE
CB_VC = CB_KC + A_KV // LANE
CB_KS = CB_VC + A_KV // LANE
CB_VS = CB_KS + A_KV // LANE
CB_KW = CB_VS + A_KV // LANE
CB_VW = CB_KW + A_KV // LANE
CB_QB = CB_VW + A_KV // LANE
CB_KB = CB_QB + B_W // LANE
CB_VB = CB_KB + B_W // LANE
CB_GATE = CB_VB + B_W // LANE

MM_TN = 512


def _cparams(sem):
    return pltpu.CompilerParams(dimension_semantics=sem, vmem_limit_bytes=VMEM_LIMIT)


def _nt_dot(a, b):
    return lax.dot_general(a, b, (((1,), (1,)), ((), ())), preferred_element_type=F32)


def _rope_lanes(x, tab):
    c = tab[:, 0:LANE]
    sa = tab[:, LANE:2 * LANE]
    sb = tab[:, 2 * LANE:3 * LANE]
    return (x * c + pltpu.roll(x, LANE - ROT_HALF, axis=1) * sa
            + pltpu.roll(x, ROT_HALF, axis=1) * sb)


def _mm_kernel(*refs, has_bias, act, has_rope, n_sub):
    if has_rope:
        modes_ref, a_ref, b_ref = refs[0], refs[1], refs[2]
        rest = refs[3:]
    else:
        a_ref, b_ref = refs[0], refs[1]
        rest = refs[2:]
    idx = 0
    if has_bias:
        bias_ref = rest[idx]
        idx += 1
    if has_rope:
        tab_ref = rest[idx]
        idx += 1
    o_ref = rest[idx]

    acc = jnp.dot(a_ref[...], b_ref[...], preferred_element_type=F32)
    if has_bias:
        acc = acc + bias_ref[...]
    if act == "gelu":
        acc = jax.nn.gelu(acc, approximate=True)
    if not has_rope:
        o_ref[...] = acc.astype(o_ref.dtype)
        return
    mode = modes_ref[pl.program_id(1)]

    @pl.when(mode == 0)
    def _():
        o_ref[...] = acc.astype(o_ref.dtype)

    @pl.when(mode != 0)
    def _():
        tab = tab_ref[...]
        for s in range(n_sub):
            sl = slice(s * LANE, (s + 1) * LANE)
            o_ref[:, sl] = _rope_lanes(acc[:, sl], tab).astype(o_ref.dtype)


def _mm(a, b, *, bias=None, act=None, rope_tabs=None, rope_modes=None,
        out_dtype=None, tm=1024, tn=MM_TN, b_layer=None, name="mm"):
    out_dtype = BF16 if out_dtype is None else out_dtype
    m, k = a.shape
    k2, n = b.shape[-2:]
    assert k == k2 and (b.ndim == 3) == (b_layer is not None)
    tm = min(tm, m)
    tn = min(tn, n)
    assert m % tm == 0 and n % tn == 0
    has_bias = bias is not None
    has_rope = rope_tabs is not None
    grid = (m // tm, n // tn)
    kern = functools.partial(_mm_kernel, has_bias=has_bias, act=act, has_rope=has_rope,
                             n_sub=tn // LANE)
    if b_layer is None:
        b_spec = pl.BlockSpec((k, tn), lambda i, j, *_: (0, j))
    else:
        b_spec = pl.BlockSpec((None, k, tn), lambda i, j, *_: (b_layer, 0, j))
    if has_rope:
        in_specs = [pl.BlockSpec((tm, k), lambda i, j, md: (i, 0)), b_spec]
        args = [a, b]
        if has_bias:
            in_specs.append(pl.BlockSpec((1, tn), lambda i, j, md: (0, j)))
            args.append(bias.reshape(1, n).astype(F32))
        in_specs.append(pl.BlockSpec((None, tm, 3 * LANE),
                                     lambda i, j, md: (jnp.maximum(md[j] - 1, 0), i, 0)))
        args.append(rope_tabs)
        gs = pltpu.PrefetchScalarGridSpec(
            num_scalar_prefetch=1, grid=grid, in_specs=in_specs,
            out_specs=pl.BlockSpec((tm, tn), lambda i, j, md: (i, j)))
        return pl.pallas_call(kern, grid_spec=gs,
                              out_shape=jax.ShapeDtypeStruct((m, n), out_dtype),
                              compiler_params=_cparams(("parallel", "arbitrary")),
                              name=name)(rope_modes, *args)
    in_specs = [pl.BlockSpec((tm, k), lambda i, j: (i, 0)), b_spec]
    args = [a, b]
    if has_bias:
        in_specs.append(pl.BlockSpec((1, tn), lambda i, j: (0, j)))
        args.append(bias.reshape(1, n).astype(F32))
    return pl.pallas_call(kern, grid=grid, in_specs=in_specs,
                          out_specs=pl.BlockSpec((tm, tn), lambda i, j: (i, j)),
                          out_shape=jax.ShapeDtypeStruct((m, n), out_dtype),
                          compiler_params=_cparams(("parallel", "arbitrary")),
                          name=name)(*args)


def _ln_kernel(x_ref, y_ref, g_ref, b_ref, of_ref, ob_ref):
    z = DEEPNORM_ALPHA * x_ref[...] + y_ref[...].astype(F32)
    mu = jnp.mean(z, axis=-1, keepdims=True)
    zc = z - mu
    var = jnp.mean(zc * zc, axis=-1, keepdims=True)
    out = zc * lax.rsqrt(var + 1e-5) * g_ref[...] + b_ref[...]
    of_ref[...] = out
    ob_ref[...] = out.astype(BF16)


def _ln_res(x, y, g, b, tm=256):
    t, d = x.shape
    tm = min(tm, t)
    row = pl.BlockSpec((tm, d), lambda i: (i, 0))
    vec = pl.BlockSpec((1, d), lambda i: (0, 0))
    return pl.pallas_call(
        _ln_kernel, grid=(t // tm,), in_specs=[row, row, vec, vec], out_specs=[row, row],
        out_shape=[jax.ShapeDtypeStruct((t, d), F32), jax.ShapeDtypeStruct((t, d), BF16)],
        compiler_params=_cparams(("parallel",)), name="ln_res",
    )(x, y, g.reshape(1, d), b.reshape(1, d))


def _rope_tables(pos, scale):
    inv = ROPE_THETA ** (-jnp.arange(ROT_HALF, dtype=F32) / ROT_HALF)
    ang = pos.astype(F32)[:, None] * inv[None, :]
    cos = jnp.cos(ang)
    sin = jnp.sin(ang)
    n = pos.shape[0]
    ones = jnp.ones((n, HEAD_DIM - ROT_DIM), F32)
    zer = jnp.zeros((n, HEAD_DIM - ROT_HALF), F32)
    c = jnp.concatenate([cos, cos, ones], axis=1)
    sa = jnp.concatenate([-sin, zer], axis=1)
    sb = jnp.concatenate([jnp.zeros((n, ROT_HALF), F32), sin,
                          jnp.zeros((n, HEAD_DIM - ROT_DIM), F32)], axis=1)
    return jnp.concatenate([c, sa, sb], axis=1) * scale


def _swa_kernel(q_ref, kp_ref, kc_ref, vp_ref, vc_ref, *out_refs, nh, shared_kv, max_dist,
                tq, with_lse, log2_scores):
    assert not (with_lse and log2_scores)
    o_ref = out_refs[0]
    i = pl.program_id(1)
    row = lax.broadcasted_iota(jnp.int32, (tq, 2 * tq), 0)
    col = lax.broadcasted_iota(jnp.int32, (tq, 2 * tq), 1)
    dist = row + tq - col
    mask = (dist >= 0) & (dist <= max_dist) & (col + (i - 1) * tq >= 0)
    for h in range(nh):
        kv = 0 if shared_kv else h
        hs = slice(h * LANE, (h + 1) * LANE)
        ks = slice(kv * LANE, (kv + 1) * LANE)
        q = q_ref[:, hs]
        k = jnp.concatenate([kp_ref[:, ks], kc_ref[:, ks]], axis=0)
        v = jnp.concatenate([vp_ref[:, ks], vc_ref[:, ks]], axis=0)
        s = jnp.where(mask, _nt_dot(q, k), NEG_INF)
        m = jnp.max(s, axis=-1, keepdims=True)
        e = jnp.exp2(s - m) if log2_scores else jnp.exp(s - m)
        den = jnp.sum(e, axis=-1, keepdims=True)
        o = jnp.dot(e.astype(BF16), v, preferred_element_type=F32) / den
        o_ref[:, hs] = o.astype(o_ref.dtype)
        if with_lse:
            out_refs[1][:, hs] = jnp.broadcast_to(m + jnp.log(den), (tq, LANE))


def _swa(qsrc, ksrc, vsrc, *, n_r, n_tiles, tq, nh, shared_kv, max_dist, q_map, k_map, v_map,
         out_cols, o_map, with_lse, log2_scores, name):
    kvw = LANE if shared_kv else nh * LANE
    qw = nh * LANE
    prev = lambda f: (lambda r, i: (jnp.maximum(i - 1, 0), f(r)))
    cur = lambda f: (lambda r, i: (i, f(r)))
    in_specs = [pl.BlockSpec((tq, qw), cur(q_map)),
                pl.BlockSpec((tq, kvw), prev(k_map)), pl.BlockSpec((tq, kvw), cur(k_map)),
                pl.BlockSpec((tq, kvw), prev(v_map)), pl.BlockSpec((tq, kvw), cur(v_map))]
    rows = n_tiles * tq
    out_shape = [jax.ShapeDtypeStruct((rows, out_cols), BF16)]
    out_specs = [pl.BlockSpec((tq, qw), cur(o_map))]
    if with_lse:
        out_shape.append(jax.ShapeDtypeStruct((rows, out_cols), F32))
        out_specs.append(pl.BlockSpec((tq, qw), cur(o_map)))
    kern = functools.partial(_swa_kernel, nh=nh, shared_kv=shared_kv, max_dist=max_dist, tq=tq,
                             with_lse=with_lse, log2_scores=log2_scores)
    return pl.pallas_call(kern, grid=(n_r, n_tiles), in_specs=in_specs, out_specs=out_specs,
                          out_shape=out_shape,
                          compiler_params=_cparams(("parallel", "arbitrary")),
                          name=name)(qsrc, ksrc, ksrc, vsrc, vsrc)


def _band_kernel(q_ref, k_ref, v_ref, o_ref, lse_ref, bias_sc, *, tq, window, dil):
    i = pl.program_id(1)
    n_delta = bias_sc.shape[0]

    @pl.when(i == 0)
    def _():
        row = lax.broadcasted_iota(jnp.int32, (tq, tq), 0)
        col = lax.broadcasted_iota(jnp.int32, (tq, tq), 1)
        for delta in range(n_delta):
            dist = row - col + delta * tq
            valid = (dist >= 0) & (dist <= window) & ((dist & (dil - 1)) == 0)
            bias_sc[delta] = jnp.where(valid, 0.0, NEG_INF)

    q = q_ref[...]

    def body(delta, carry):
        m, l, acc = carry
        start = pl.multiple_of((i - delta) * tq, tq)
        k = k_ref[pl.ds(start, tq), :]
        v = v_ref[pl.ds(start, tq), :]
        s = _nt_dot(q, k) + bias_sc[delta]
        m_new = jnp.maximum(m, jnp.max(s, axis=-1, keepdims=True))
        alpha = jnp.exp(m - m_new)
        p = jnp.exp(s - m_new)
        l = alpha * l + jnp.sum(p, axis=-1, keepdims=True)
        acc = alpha * acc + jnp.dot(p.astype(BF16), v, preferred_element_type=F32)
        return m_new, l, acc

    init = (jnp.full((tq, 1), NEG_INF, F32), jnp.zeros((tq, 1), F32), jnp.zeros((tq, LANE), F32))
    m, l, acc = lax.fori_loop(0, jnp.minimum(i + 1, n_delta), body, init)
    o_ref[...] = (acc / l).astype(o_ref.dtype)
    lse_ref[...] = jnp.broadcast_to(m + jnp.log(l), (tq, LANE))


def _band_attn(h, t, *, q_cb, k_cb, v_cb, nh, window, dil, tq=256):
    tq = min(tq, t)
    assert dil & (dil - 1) == 0
    kern = functools.partial(_band_kernel, tq=tq, window=window, dil=dil)
    blk = pl.BlockSpec((tq, LANE), lambda hd, i: (i, hd))
    return pl.pallas_call(
        kern, grid=(nh, t // tq),
        in_specs=[pl.BlockSpec((tq, LANE), lambda hd, i: (i, q_cb + hd)),
                  pl.BlockSpec((t, LANE), lambda hd, i: (0, k_cb + hd)),
                  pl.BlockSpec((t, LANE), lambda hd, i: (0, v_cb + hd))],
        out_specs=[blk, blk],
        out_shape=[jax.ShapeDtypeStruct((t, nh * LANE), BF16), jax.ShapeDtypeStruct((t, nh * LANE), F32)],
        scratch_shapes=[pltpu.VMEM((window // tq + 1, tq, tq), F32)],
        compiler_params=_cparams(("arbitrary", "arbitrary")), name="dilated_band_%d" % dil,
    )(h, h, h)


def _cmp_kernel(q_ref, kc_ref, vc_ref, ov_ref, oc_ref, sel_ref, *, tq, n_cmp_pad, n_slc):
    i = pl.program_id(1)
    tqv = i * tq + lax.broadcasted_iota(jnp.int32, (tq, 1), 0)
    cmp_end = CMP_STRIDE * lax.broadcasted_iota(jnp.int32, (1, n_cmp_pad), 1) + (CMP_LEN - 1)
    maskc = cmp_end <= tqv
    kc = kc_ref[...]
    vc = vc_ref[...]
    psum = jnp.zeros((tq, n_cmp_pad), F32)
    for h in range(A_HPG):
        hs = slice(h * LANE, (h + 1) * LANE)
        s = jnp.where(maskc, _nt_dot(q_ref[:, hs], kc), NEG_INF)
        m = jnp.max(s, axis=-1, keepdims=True)
        e = jnp.where(maskc, jnp.exp2(s - m), 0.0)
        den = jnp.sum(e, axis=-1, keepdims=True)
        p = e / jnp.maximum(den, 1e-30)
        oc_ref[:, hs] = jnp.dot(p.astype(BF16), vc, preferred_element_type=F32).astype(oc_ref.dtype)
        psum = psum + p
    ov = ov_ref[...]
    p1 = psum.astype(BF16)
    r1 = psum - p1.astype(F32)
    p2 = r1.astype(BF16)
    p3 = (r1 - p2.astype(F32)).astype(BF16)
    imp = (jnp.dot(p1, ov, preferred_element_type=F32) + jnp.dot(p2, ov, preferred_element_type=F32)
           + jnp.dot(p3, ov, preferred_element_type=F32))
    j = lax.broadcasted_iota(jnp.int32, (1, n_slc), 1)
    jf = j.astype(F32)
    tb = tqv // SLC_LEN
    forced = (j == 0) | (j == tb) | (j == tb - 1)
    val = jnp.where(forced, SEL_FORCE, jnp.where(j <= tb, imp, -SEL_FORCE))
    sel = jnp.zeros((tq, n_slc), F32)
    for _ in range(min(SLC_TOPK, n_slc)):
        m = jnp.max(val, axis=-1, keepdims=True)
        first = jnp.min(jnp.where(val == m, jf, float(n_slc)), axis=-1, keepdims=True)
        hit = jf == first
        sel = jnp.where(hit & (m > -0.5 * SEL_FORCE), 1.0, sel)
        val = jnp.where(hit, -3e38, val)
    sel_ref[...] = sel.astype(sel_ref.dtype)


def _cmp_attn(h, kc, vc, t, tq=512):
    n_cmp_pad = kc.shape[0] // A_KV_GROUPS
    n_slc = t // SLC_LEN
    ci = np.arange(n_cmp_pad)[:, None] * CMP_STRIDE
    sj = np.arange(n_slc)[None, :] * SLC_LEN
    overlap = ((ci < sj + SLC_LEN) & (ci + CMP_LEN > sj) & (np.arange(n_cmp_pad)[:, None] < t // CMP_STRIDE - 1))
    overlap = jnp.asarray(overlap.astype(np.float32), BF16)
    kern = functools.partial(_cmp_kernel, tq=tq, n_cmp_pad=n_cmp_pad, n_slc=n_slc)
    qw = A_HPG * LANE
    return pl.pallas_call(
        kern, grid=(A_KV_GROUPS, t // tq),
        in_specs=[pl.BlockSpec((tq, qw), lambda g, i: (i, g)),
                  pl.BlockSpec((n_cmp_pad, LANE), lambda g, i: (g, 0)),
                  pl.BlockSpec((n_cmp_pad, LANE), lambda g, i: (g, 0)),
                  pl.BlockSpec((n_cmp_pad, n_slc), lambda g, i: (0, 0))],
        out_specs=[pl.BlockSpec((tq, qw), lambda g, i: (i, g)),
                   pl.BlockSpec((tq, n_slc), lambda g, i: (i, g))],
        out_shape=[jax.ShapeDtypeStruct((t, A_Q), BF16),
                   jax.ShapeDtypeStruct((t, A_KV_GROUPS * n_slc), BF16)],
        compiler_params=_cparams(("parallel", "arbitrary")), name="cmp_attn_topk",
    )(h, kc, vc, overlap)


ROW_CHUNK = 32


def _flash_pipeline(n, score_stage, softmax_stage, value_stage):
    @pl.when(n > 0)
    def _():
        score_stage(0, 0)

    def step(t, slot):
        softmax_stage(slot)
        value_stage(1 - slot, jnp.maximum(t - 1, 0))
        score_stage(1 - slot, jnp.minimum(t + 1, n - 1))

    def pair(u, c):
        step(2 * u, 0)
        step(2 * u + 1, 1)
        return c

    lax.fori_loop(0, n // 2, pair, 0)
    last = jnp.maximum(n - 1, 0)

    @pl.when(n % 2 == 1)
    def _():
        step(n - 1, 0)
        value_stage(0, last)

    @pl.when(n % 2 == 0)
    def _():
        value_stage(1, last)


def _sel_kernel(q_ref, k_ref, v_ref, sel_ref, o_ref, q5_sc, s_sc, bias_sc, p_sc, acc_sc, m_sc, l_sc,
                alpha_sc, *, tq, tk, n_slc):
    i = pl.program_id(1)
    nh = A_HPG
    for h in range(nh):
        q5_sc[h * tq:(h + 1) * tq, :] = q_ref[:, h * LANE:(h + 1) * LANE]
    m_sc[...] = jnp.full(m_sc.shape, NEG_INF, F32)
    l_sc[...] = jnp.zeros_like(l_sc)
    acc_sc[...] = jnp.zeros_like(acc_sc)
    p_sc[1] = jnp.zeros(p_sc.shape[1:], p_sc.dtype)
    alpha_sc[1] = jnp.ones(alpha_sc.shape[1:], F32)
    selb = sel_ref[...]
    tqv = i * tq + lax.broadcasted_iota(jnp.int32, (tq, 1), 0)
    blk_per_tile = tk // SLC_LEN
    blk_gap = (lax.broadcasted_iota(jnp.int32, (n_slc, tk), 0)
               - lax.broadcasted_iota(jnp.int32, (n_slc, tk), 1) // SLC_LEN)
    kcol = lax.broadcasted_iota(jnp.int32, (1, tk), 1)
    n_kv = (i * tq + tq + tk - 1) // tk
    nrep = tk // LANE

    def score_stage(slot, t):
        k = k_ref[pl.ds(pl.multiple_of(t * tk, tk), tk), :]
        s_sc[slot] = _nt_dot(q5_sc[...], k)
        expand = jnp.where(blk_gap == t * blk_per_tile, 1.0, 0.0).astype(BF16)
        picked = jnp.dot(selb, expand, preferred_element_type=F32)
        bias_sc[slot] = jnp.where((picked > 0.5) & (kcol + t * tk <= tqv), 0.0, NEG_INF)

    def softmax_stage(slot):
        for c in range(nh * tq // ROW_CHUNK):
            rows = slice(c * ROW_CHUNK, (c + 1) * ROW_CHUNK)
            r0 = (c * ROW_CHUNK) % tq
            s = s_sc[slot, rows, :] + bias_sc[slot, r0:r0 + ROW_CHUNK, :]
            m_old = m_sc[rows, :]
            m_new = jnp.maximum(m_old, jnp.max(s, axis=-1, keepdims=True))
            alpha = jnp.exp2(m_old - m_new)
            p = jnp.exp2(s - jnp.concatenate([m_new] * nrep, axis=1))
            l_sc[rows, :] = alpha * l_sc[rows, :] + jnp.sum(p, axis=-1, keepdims=True)
            m_sc[rows, :] = m_new
            alpha_sc[slot, rows, :] = alpha
            p_sc[slot, rows, :] = p.astype(p_sc.dtype)

    def value_stage(slot, t):
        v = v_ref[pl.ds(pl.multiple_of(t * tk, tk), tk), :]
        acc_sc[...] = acc_sc[...] * alpha_sc[slot] + jnp.dot(p_sc[slot], v, preferred_element_type=F32)

    _flash_pipeline(n_kv, score_stage, softmax_stage, value_stage)
    o = acc_sc[...] / l_sc[...]
    for h in range(nh):
        o_ref[:, h * LANE:(h + 1) * LANE] = o[h * tq:(h + 1) * tq].astype(o_ref.dtype)


def _sel_attn(h, sel, t, tq=128, tk=512):
    n_slc = t // SLC_LEN
    tk = min(tk, t)
    qw = A_HPG * LANE
    rows = A_HPG * tq
    kern = functools.partial(_sel_kernel, tq=tq, tk=tk, n_slc=n_slc)
    return pl.pallas_call(
        kern, grid=(A_KV_GROUPS, t // tq),
        in_specs=[pl.BlockSpec((tq, qw), lambda g, i: (i, g)),
                  pl.BlockSpec((t, LANE), lambda g, i: (0, CB_KS + g)),
                  pl.BlockSpec((t, LANE), lambda g, i: (0, CB_VS + g)),
                  pl.BlockSpec((tq, n_slc), lambda g, i: (i, g))],
        out_specs=pl.BlockSpec((tq, qw), lambda g, i: (i, g)),
        out_shape=jax.ShapeDtypeStruct((t, A_Q), BF16),
        scratch_shapes=[pltpu.VMEM((rows, LANE), BF16), pltpu.VMEM((2, rows, tk), F32),
                        pltpu.VMEM((2, tq, tk), F32), pltpu.VMEM((2, rows, tk), BF16),
                        pltpu.VMEM((rows, LANE), F32), pltpu.VMEM((rows, LANE), F32),
                        pltpu.VMEM((rows, LANE), F32), pltpu.VMEM((2, rows, LANE), F32)],
        compiler_params=_cparams(("parallel", "arbitrary")), name="sel_attn",
    )(h, h, h, sel)


def _mixprep_kernel(oc_ref, os_ref, ow_ref, gate_ref, ob_ref, lse_ref, out_ref):
    gate = jax.nn.sigmoid(gate_ref[...].astype(F32))
    for h in range(A_HEADS):
        hs = slice(h * LANE, (h + 1) * LANE)
        o = (gate[:, 3 * h:3 * h + 1] * oc_ref[:, hs].astype(F32)
             + gate[:, 3 * h + 1:3 * h + 2] * os_ref[:, hs].astype(F32)
             + gate[:, 3 * h + 2:3 * h + 3] * ow_ref[:, hs].astype(F32))
        out_ref[:, hs] = o.astype(out_ref.dtype)
    ng = len(B_DILATIONS)
    for hi in range(B_HPG):
        lses = [lse_ref[:, (g * B_HPG + hi) * LANE:(g * B_HPG + hi + 1) * LANE] for g in range(ng)]
        m = functools.reduce(jnp.maximum, lses)
        es = [jnp.exp(x - m) for x in lses]
        den = functools.reduce(lambda a, b: a + b, es)
        for g in range(ng):
            src = slice((g * B_HPG + hi) * LANE, (g * B_HPG + hi + 1) * LANE)
            dst = slice(A_Q + (g * B_HPG + hi) * LANE, A_Q + (g * B_HPG + hi + 1) * LANE)
            out_ref[:, dst] = (ob_ref[:, src].astype(F32) * (es[g] / den)).astype(out_ref.dtype)


def _mixprep(o_c, o_s, o_w, h, o_b, lse_b, t, tq=256):
    tq = min(tq, t)
    a = pl.BlockSpec((tq, A_Q), lambda i: (i, 0))
    b = pl.BlockSpec((tq, B_W), lambda i: (i, 0))
    return pl.pallas_call(
        _mixprep_kernel, grid=(t // tq,),
        in_specs=[a, a, a, pl.BlockSpec((tq, LANE), lambda i: (i, CB_GATE)), b, b],
        out_specs=pl.BlockSpec((tq, A_Q + B_W), lambda i: (i, 0)),
        out_shape=jax.ShapeDtypeStruct((t, A_Q + B_W), BF16),
        compiler_params=_cparams(("parallel",)), name="even_mixprep",
    )(o_c, o_s, o_w, h, o_b, lse_b)


def _diff_kernel(q_ref, k_ref, v_ref, lq1_ref, lk1_ref, lq2_ref, lk2_ref, g_ref, o_ref,
                 s_sc, p_sc, acc_sc, m_sc, l_sc, alpha_sc, *, tq, tk, lam_init):
    i = pl.program_id(1)
    q1 = q_ref[:, 0:LANE]
    q2 = q_ref[:, LANE:2 * LANE]
    lam = (jnp.exp(jnp.sum(lq1_ref[...] * lk1_ref[...], axis=-1, keepdims=True))
           - jnp.exp(jnp.sum(lq2_ref[...] * lk2_ref[...], axis=-1, keepdims=True)) + lam_init)

    m_sc[...] = jnp.full(m_sc.shape, NEG_INF, F32)
    l_sc[...] = jnp.zeros_like(l_sc)
    acc_sc[...] = jnp.zeros_like(acc_sc)
    p_sc[1] = jnp.zeros(p_sc.shape[1:], p_sc.dtype)
    alpha_sc[1] = jnp.ones(alpha_sc.shape[1:], F32)
    nrep = tk // LANE
    rep = C_VDIM // LANE

    def score_stage(slot, start):
        k = k_ref[pl.ds(start, tk), :]
        s_sc[slot, 0] = _nt_dot(q1, k[:, 0:LANE])
        s_sc[slot, 1] = _nt_dot(q2, k[:, LANE:2 * LANE])

    def softmax_stage(slot, diag_off):
        for hd in range(2):
            for c in range(tq // ROW_CHUNK):
                rows = slice(c * ROW_CHUNK, (c + 1) * ROW_CHUNK)
                s = s_sc[slot, hd, rows, :]
                if diag_off is not None:
                    row = lax.broadcasted_iota(jnp.int32, (ROW_CHUNK, tk), 0) + c * ROW_CHUNK
                    col = lax.broadcasted_iota(jnp.int32, (ROW_CHUNK, tk), 1) + diag_off
                    s = jnp.where(col <= row, s, NEG_INF)
                m_old = m_sc[hd, rows, :]
                m_new = jnp.maximum(m_old, jnp.max(s, axis=-1, keepdims=True))
                alpha = jnp.exp2(m_old - m_new)
                p = jnp.exp2(s - jnp.concatenate([m_new] * nrep, axis=1))
                l_sc[hd, rows, :] = alpha * l_sc[hd, rows, :] + jnp.sum(p, axis=-1, keepdims=True)
                m_sc[hd, rows, :] = m_new
                alpha_sc[slot, hd, rows, :] = alpha
                p_sc[slot, hd, rows, :] = p.astype(p_sc.dtype)

    def value_stage(slot, start):
        v = v_ref[pl.ds(start, tk), :]
        for hd in range(2):
            pv = jnp.dot(p_sc[slot, hd], v, preferred_element_type=F32)
            alpha = alpha_sc[slot, hd]
            acc_sc[hd] = acc_sc[hd] * jnp.concatenate([alpha] * rep, axis=1) + pv

    per = tq // tk
    n_full = i * per
    _flash_pipeline(n_full, lambda slot, t: score_stage(slot, pl.multiple_of(t * tk, tk)),
                    lambda slot: softmax_stage(slot, None),
                    lambda slot, t: value_stage(slot, pl.multiple_of(t * tk, tk)))
    for u in range(per):
        start = pl.multiple_of(i * tq + u * tk, tk)
        score_stage(0, start)
        softmax_stage(0, u * tk)
        value_stage(0, start)
    l1 = jnp.concatenate([l_sc[0]] * rep, axis=1)
    l2 = jnp.concatenate([l_sc[1]] * rep, axis=1)
    o = acc_sc[0] / l1 - lam * (acc_sc[1] / l2)
    o = o * lax.rsqrt(jnp.mean(o * o, axis=-1, keepdims=True) + 1e-5) * g_ref[...] * (1.0 - lam_init)
    o_ref[...] = o.astype(o_ref.dtype)


def _diff_attn(h, lq1, lk1, lq2, lk2, sub_g, layer, t, tq=512, tk=512):
    tq = min(tq, t)
    tk = min(tk, tq)
    lam_init = 0.8 - 0.6 * math.exp(-0.3 * layer)
    kern = functools.partial(_diff_kernel, tq=tq, tk=tk, lam_init=lam_init)
    vec = pl.BlockSpec((1, LANE), lambda hh, i: (0, 0))
    qkb = C_QK // C_VDIM
    return pl.pallas_call(
        kern, grid=(C_HEADS, t // tq),
        in_specs=[pl.BlockSpec((tq, C_VDIM), lambda hh, i: (i, hh)),
                  pl.BlockSpec((t, C_VDIM), lambda hh, i: (0, qkb + hh)),
                  pl.BlockSpec((t, C_VDIM), lambda hh, i: (0, 2 * qkb + hh)),
                  vec, vec, vec, vec, pl.BlockSpec((1, C_VDIM), lambda hh, i: (0, 0))],
        out_specs=pl.BlockSpec((tq, C_VDIM), lambda hh, i: (i, hh)),
        out_shape=jax.ShapeDtypeStruct((t, C_V), BF16),
        scratch_shapes=[pltpu.VMEM((2, 2, tq, tk), F32), pltpu.VMEM((2, 2, tq, tk), BF16),
                        pltpu.VMEM((2, tq, C_VDIM), F32), pltpu.VMEM((2, tq, LANE), F32),
                        pltpu.VMEM((2, tq, LANE), F32), pltpu.VMEM((2, 2, tq, LANE), F32)],
        compiler_params=_cparams(("parallel", "arbitrary")), name="diff_attn",
    )(h, h, h, lq1.reshape(1, LANE).astype(F32), lk1.reshape(1, LANE).astype(F32),
      lq2.reshape(1, LANE).astype(F32), lk2.reshape(1, LANE).astype(F32),
      sub_g.reshape(1, C_VDIM).astype(F32))


ROUTE_EID = 0
ROUTE_RANK = 4
ROUTE_W = 8
MOE_TM = 256


def _router_kernel(x_ref, whi_ref, wlo_ref, b_ref, route_ref, cnt_ref, carry_ref):
    i = pl.program_id(0)

    @pl.when(i == 0)
    def _():
        carry_ref[...] = jnp.zeros_like(carry_ref)

    x = x_ref[...]
    x_hi = x.astype(BF16)
    x_lo = (x - x_hi.astype(F32)).astype(BF16)
    logits = (jnp.dot(x_hi, whi_ref[...], preferred_element_type=F32)
              + jnp.dot(x_lo, whi_ref[...], preferred_element_type=F32)
              + jnp.dot(x_hi, wlo_ref[...], preferred_element_type=F32)) + b_ref[...]
    tm, ne = logits.shape
    lane = lax.broadcasted_iota(jnp.int32, (1, ne), 1)
    jf = lane.astype(F32)
    val = logits
    tops, hits, firsts = [], [], []
    for _ in range(TOP_K):
        m = jnp.max(val, axis=-1, keepdims=True)
        first = jnp.min(jnp.where(val == m, jf, float(ne)), axis=-1, keepdims=True)
        hit = jf == first
        tops.append(m)
        hits.append(hit)
        firsts.append(first)
        val = jnp.where(hit, -3e38, val)
    es = [jnp.exp(x - tops[0]) for x in tops]
    den = functools.reduce(lambda a, b: a + b, es)
    picked = jnp.zeros((tm, ne), F32)
    for hit in hits:
        picked = jnp.where(hit, 1.0, picked)
    row = lax.broadcasted_iota(jnp.int32, (tm, tm), 0)
    col = lax.broadcasted_iota(jnp.int32, (tm, tm), 1)
    tri = jnp.where(row > col, 1.0, 0.0).astype(BF16)
    rank = jnp.dot(tri, picked.astype(BF16), preferred_element_type=F32) + carry_ref[...]
    carry_ref[...] += jnp.sum(picked, axis=0, keepdims=True)
    cnt_ref[...] = carry_ref[...]
    route = jnp.zeros((tm, ne), F32)
    for k in range(TOP_K):
        rk = jnp.sum(jnp.where(hits[k], rank, 0.0), axis=-1, keepdims=True)
        route = jnp.where(lane == ROUTE_EID + k, firsts[k], route)
        route = jnp.where(lane == ROUTE_RANK + k, rk, route)
        route = jnp.where(lane == ROUTE_W + k, es[k] / den, route)
    route_ref[...] = route


def _router(x, rw, rb, tm=256):
    t, d = x.shape
    tm = min(tm, t)
    rw_p = jnp.pad(rw.astype(F32), ((0, 0), (0, LANE - N_EXPERTS)))
    rb_p = jnp.pad(rb.astype(F32), (0, LANE - N_EXPERTS), constant_values=NEG_INF).reshape(1, LANE)
    w_hi = rw_p.astype(BF16)
    w_lo = (rw_p - w_hi.astype(F32)).astype(BF16)
    wblk = pl.BlockSpec((d, LANE), lambda i: (0, 0))
    return pl.pallas_call(
        _router_kernel, grid=(t // tm,),
        in_specs=[pl.BlockSpec((tm, d), lambda i: (i, 0)), wblk, wblk,
                  pl.BlockSpec((1, LANE), lambda i: (0, 0))],
        out_specs=[pl.BlockSpec((tm, LANE), lambda i: (i, 0)), pl.BlockSpec((1, LANE), lambda i: (0, 0))],
        out_shape=[jax.ShapeDtypeStruct((t, LANE), F32), jax.ShapeDtypeStruct((1, LANE), F32)],
        scratch_shapes=[pltpu.VMEM((1, LANE), F32)],
        compiler_params=_cparams(("arbitrary",)), name="moe_router",
    )(x, w_hi, w_lo, rb_p)


def _row_copy(src, src_row, dst, dst_row, sem):
    return pltpu.make_async_copy(src.at[pl.ds(src_row, 1)], dst.at[pl.ds(dst_row, 1)], sem)


def _dispatch_kernel(gend_ref, dest_ref, x_ref, xs_hbm, zeros_ref, sem, *, tm):
    i = pl.program_id(0)

    @pl.when(i == 0)
    def _():
        zeros_ref[...] = jnp.zeros_like(zeros_ref)

        def clear(e):
            start = pl.multiple_of(gend_ref[e] - MOE_TM, MOE_TM)
            return pltpu.make_async_copy(zeros_ref, xs_hbm.at[pl.ds(start, MOE_TM)], sem)
        for e in range(N_EXPERTS):
            clear(e).start()
        for e in range(N_EXPERTS):
            clear(e).wait()

        def clear_tail(j, c):
            cp = pltpu.make_async_copy(
                zeros_ref, xs_hbm.at[pl.ds(pl.multiple_of(j * MOE_TM, MOE_TM), MOE_TM)], sem)
            cp.start()
            cp.wait()
            return c

        lax.fori_loop(gend_ref[N_EXPERTS - 1] // MOE_TM, xs_hbm.shape[0] // MOE_TM, clear_tail, 0)

    def issue(r, c):
        for k in range(TOP_K):
            _row_copy(x_ref, r, xs_hbm, dest_ref[TOP_K * r + k], sem).start()
        return c

    lax.fori_loop(0, tm, issue, 0)

    def drain(r, c):
        for k in range(TOP_K):
            _row_copy(x_ref, 0, xs_hbm, 0, sem).wait()
        return c

    lax.fori_loop(0, tm, drain, 0)


def _dispatch(xf, dest_flat, gend, n_rows, tm=256):
    t, d = xf.shape
    tm = min(tm, t)
    gs = pltpu.PrefetchScalarGridSpec(
        num_scalar_prefetch=1, grid=(t // tm,),
        in_specs=[pl.BlockSpec((TOP_K * tm,), lambda i, ge: (i,), memory_space=pltpu.SMEM),
                  pl.BlockSpec((tm, d), lambda i, ge: (i, 0))],
        out_specs=pl.BlockSpec(memory_space=pl.ANY),
        scratch_shapes=[pltpu.VMEM((MOE_TM, d), F32), pltpu.SemaphoreType.DMA(())])
    return pl.pallas_call(
        functools.partial(_dispatch_kernel, tm=tm), grid_spec=gs, out_shape=jax.ShapeDtypeStruct((n_rows, d), F32),
        compiler_params=_cparams(("arbitrary",)), name="moe_dispatch",
    )(gend, dest_flat, xf)


def _expert_kernel(eid_ref, nused_ref, xs_ref, wgu_ref, bgu_ref, wd_ref, bd_ref, pick_ref, ys_ref):
    j = pl.program_id(0)

    @pl.when(j < nused_ref[0])
    def _():
        x = xs_ref[...].astype(BF16)
        hgu = jnp.dot(x, wgu_ref[...], preferred_element_type=F32) + bgu_ref[...]
        glu = jnp.minimum(hgu, SWIGLU_LIMIT)
        gated = glu * jax.nn.sigmoid(SWIGLU_ALPHA * glu)
        lin = jnp.clip(hgu, -SWIGLU_LIMIT, SWIGLU_LIMIT) + 1.0
        prod = jnp.concatenate(
            [gated[:, s:s + LANE] * pltpu.roll(lin[:, s:s + LANE], LANE - 1, axis=1)
             for s in range(0, 2 * D_EXPERT, LANE)], axis=1)
        act = jnp.dot(prod.astype(BF16), pick_ref[...], preferred_element_type=F32)
        ys_ref[...] = jnp.dot(act.astype(BF16), wd_ref[...], preferred_element_type=F32) + bd_ref[...]

    @pl.when(j >= nused_ref[0])
    def _():
        ys_ref[...] = jnp.zeros_like(ys_ref)


def _experts(xs, tile_eid, nused, wgu, bgu, wd, bd, layer):
    n_rows, d = xs.shape
    n_tiles = n_rows // MOE_TM
    pick = np.zeros((2 * D_EXPERT, D_EXPERT), np.float32)
    pick[2 * np.arange(D_EXPERT), np.arange(D_EXPERT)] = 1.0
    tile = lambda j, eid, nu: (jnp.minimum(j, nu[0] - 1), 0)
    per_expert = lambda j, eid, nu: (layer, eid[j], 0, 0)
    gs = pltpu.PrefetchScalarGridSpec(
        num_scalar_prefetch=2, grid=(n_tiles,),
        in_specs=[pl.BlockSpec((MOE_TM, d), tile),
                  pl.BlockSpec((None, None, d, 2 * D_EXPERT), per_expert),
                  pl.BlockSpec((None, None, 1, 2 * D_EXPERT), per_expert),
                  pl.BlockSpec((None, None, D_EXPERT, d), per_expert),
                  pl.BlockSpec((None, None, 1, d), per_expert),
                  pl.BlockSpec((2 * D_EXPERT, D_EXPERT), lambda j, eid, nu: (0, 0))],
        out_specs=pl.BlockSpec((MOE_TM, d), lambda j, eid, nu: (j, 0)))
    return pl.pallas_call(
        _expert_kernel, grid_spec=gs, out_shape=jax.ShapeDtypeStruct((n_rows, d), F32),
        compiler_params=_cparams(("arbitrary",)), name="moe_experts",
    )(tile_eid, nused, xs, wgu, bgu, wd, bd, jnp.asarray(pick, BF16))


def _combine_ln_kernel(dest_cur_ref, dest_nxt_ref, x_ref, route_ref, g_ref, b_ref, ys_hbm,
                       of_ref, ob_ref, ybuf, sems, *, tm):
    i = pl.program_id(0)
    slot = i % 2

    def gather(dest_ref, s):
        def issue(r, c):
            for k in range(TOP_K):
                idx = TOP_K * r + k
                src_row = dest_ref[idx // LANE, idx % LANE]
                _row_copy(ys_hbm, src_row, ybuf.at[s, k], r, sems.at[s]).start()
            return c
        lax.fori_loop(0, tm, issue, 0)

    @pl.when(i == 0)
    def _():
        gather(dest_cur_ref, slot)

    @pl.when(i + 1 < pl.num_programs(0))
    def _():
        gather(dest_nxt_ref, 1 - slot)

    def drain(r, c):
        for k in range(TOP_K):
            _row_copy(ys_hbm, 0, ybuf.at[slot, k], 0, sems.at[slot]).wait()
        return c

    lax.fori_loop(0, tm, drain, 0)
    route = route_ref[...]
    z = DEEPNORM_ALPHA * x_ref[...]
    for k in range(TOP_K):
        z = z + route[:, ROUTE_W + k:ROUTE_W + k + 1] * ybuf[slot, k]
    mu = jnp.mean(z, axis=-1, keepdims=True)
    zc = z - mu
    var = jnp.mean(zc * zc, axis=-1, keepdims=True)
    out = zc * lax.rsqrt(var + 1e-5) * g_ref[...] + b_ref[...]
    of_ref[...] = out
    ob_ref[...] = out.astype(BF16)


def _combine_ln(xf, ys, dest_flat, route, g, b, tm=128):
    t, d = xf.shape
    tm = min(tm, t)
    n = t // tm
    row = pl.BlockSpec((tm, d), lambda i: (i, 0))
    lanes = pl.BlockSpec((tm, LANE), lambda i: (i, 0))
    vec = pl.BlockSpec((1, d), lambda i: (0, 0))
    kern = functools.partial(_combine_ln_kernel, tm=tm)
    idx_rows = TOP_K * tm // LANE
    dest3 = dest_flat.reshape(n, idx_rows, LANE)
    idx_blk = lambda f: pl.BlockSpec((None, idx_rows, LANE), f, memory_space=pltpu.SMEM)
    return pl.pallas_call(
        kern, grid=(n,),
        in_specs=[idx_blk(lambda i: (i, 0, 0)), idx_blk(lambda i: (jnp.minimum(i + 1, n - 1), 0, 0)),
                  row, lanes, vec, vec, pl.BlockSpec(memory_space=pl.ANY)],
        out_specs=[row, row],
        out_shape=[jax.ShapeDtypeStruct((t, d), F32), jax.ShapeDtypeStruct((t, d), BF16)],
        scratch_shapes=[pltpu.VMEM((2, TOP_K, tm, d), F32), pltpu.SemaphoreType.DMA((2,))],
        compiler_params=_cparams(("arbitrary",)), name="moe_combine_ln",
    )(dest3, dest3, xf, route, g.reshape(1, d), b.reshape(1, d), ys)


def _moe_ln(xf, rw, rb, wgu, bgu, wd, bd, g, b, layer):
    t, d = xf.shape
    route, cnt = _router(xf, rw, rb)
    cnt = cnt[0, :N_EXPERTS].astype(jnp.int32)
    tiles_e = jnp.maximum((cnt + MOE_TM - 1) // MOE_TM, 1)
    gend = jnp.cumsum(tiles_e) * MOE_TM
    gstart = gend - tiles_e * MOE_TM
    eid = route[:, ROUTE_EID:ROUTE_EID + TOP_K].astype(jnp.int32)
    rank = route[:, ROUTE_RANK:ROUTE_RANK + TOP_K].astype(jnp.int32)
    onehot = eid[..., None] == jnp.arange(N_EXPERTS, dtype=jnp.int32)
    dest = jnp.sum(jnp.where(onehot, gstart, 0), axis=-1) + rank
    dest_flat = dest.reshape(-1).astype(jnp.int32)
    n_tiles = (t * TOP_K) // MOE_TM + N_EXPERTS
    tile_row = jnp.arange(n_tiles, dtype=jnp.int32)[:, None] * MOE_TM
    tile_eid = jnp.minimum(jnp.sum((gend[None, :] <= tile_row).astype(jnp.int32), axis=1), N_EXPERTS - 1)
    nused = (gend[-1:] // MOE_TM).astype(jnp.int32)
    xs = _dispatch(xf, dest_flat, gend.astype(jnp.int32), n_tiles * MOE_TM)
    ys = _experts(xs, tile_eid, nused, wgu, bgu, wd, bd, layer)
    return _combine_ln(xf, ys, dest_flat, route, g, b)


def _compress(kv, pe, w1, b1, w2, b2, t, rope_tab):
    g = A_KV_GROUPS
    nch = t // CMP_STRIDE
    n_pad = nch
    chunks = kv.reshape(nch, CMP_STRIDE, g, HEAD_DIM)
    blocks = jnp.concatenate([chunks[:-1], chunks[1:]], axis=1)
    flat = blocks.transpose(2, 0, 1, 3).reshape(g, nch - 1, CMP_LEN * HEAD_DIM)
    flat = jnp.pad(flat, ((0, 0), (0, 1), (0, 0))).reshape(g * n_pad, CMP_LEN * HEAD_DIM)
    w1b = w1.astype(BF16)
    pe_rows = jnp.zeros((8, CMP_LEN * HEAD_DIM), F32).at[0].set(pe.reshape(-1)).astype(BF16)
    pe_term = _mm(pe_rows, w1b, out_dtype=F32, name="cmp_pe")[0]
    hid = _mm(flat, w1b, bias=pe_term + b1, act="gelu", name="cmp_mlp1")
    if rope_tab is None:
        return _mm(hid, w2.astype(BF16), bias=b2, tn=LANE, name="cmp_mlp2")
    return _mm(hid, w2.astype(BF16), bias=b2, tn=LANE, rope_tabs=rope_tab[None],
               rope_modes=jnp.ones((1,), jnp.int32), name="cmp_mlp2_rope")


def _even_w_kernel(a_ref, b_ref, o_ref, *, first_shifted, gate_tile):
    j = pl.program_id(1)
    a = a_ref[...]
    tn = a.shape[1]

    @pl.when(j < first_shifted)
    def _():
        o_ref[...] = a.astype(o_ref.dtype)

    @pl.when((j >= first_shifted) & (j < gate_tile))
    def _():
        b = b_ref[...]
        o_ref[...] = jnp.concatenate([a[:, A_GATE:], b[:, :A_GATE]], axis=1).astype(o_ref.dtype)

    @pl.when(j == gate_tile)
    def _():
        lane = lax.broadcasted_iota(jnp.int32, a.shape, 1)
        o_ref[...] = jnp.where(lane < A_GATE, a, 0.0).astype(o_ref.dtype)


def _even_w_layout(w_in, e, tr=512):
    d = w_in.shape[1]
    tr = min(tr, d)
    first_shifted = (A_Q + 6 * A_KV) // MM_TN
    gate_tile = EVEN_COLS // MM_TN - 1
    last_in = (w_in.shape[2] - 1) // MM_TN
    kern = functools.partial(_even_w_kernel, first_shifted=first_shifted, gate_tile=gate_tile)
    return pl.pallas_call(
        kern, grid=(d // tr, EVEN_COLS // MM_TN),
        in_specs=[pl.BlockSpec((None, tr, MM_TN),
                               lambda i, j: (e, i, jnp.where(j == gate_tile, first_shifted, j))),
                  pl.BlockSpec((None, tr, MM_TN), lambda i, j: (e, i, jnp.minimum(j + 1, last_in)))],
        out_specs=pl.BlockSpec((tr, MM_TN), lambda i, j: (i, j)),
        out_shape=jax.ShapeDtypeStruct((d, EVEN_COLS), BF16),
        compiler_params=_cparams(("parallel", "arbitrary")), name="even_w_layout",
    )(w_in, w_in)


def _even_mixer(xb, w_in, w_out, e, cmpk, cmpv, tabs):
    t, d = xb.shape
    w = _even_w_layout(w_in, e)
    tile_modes = np.zeros((EVEN_COLS // MM_TN,), np.int32)
    per = MM_TN // LANE
    for cb, nblk, mode in ((CB_QA, A_Q // LANE, 3), (CB_KS, A_KV // LANE, 1), (CB_KW, A_KV // LANE, 1),
                           (CB_QB, B_W // LANE, 2), (CB_KB, B_W // LANE, 1)):
        tile_modes[cb // per:(cb + nblk) // per] = mode
    h = _mm(xb, w, rope_tabs=tabs["qk"], rope_modes=jnp.asarray(tile_modes), name="even_in_proj")

    n_pad = t // CMP_STRIDE
    kc = _compress(h[:, CB_KC * LANE:CB_VC * LANE], *cmpk, t, jnp.tile(tabs["cmp"], (A_KV_GROUPS, 1)))
    vc = _compress(h[:, CB_VC * LANE:CB_KS * LANE], *cmpv, t, None)
    o_c, sel = _cmp_attn(h, kc, vc, t)
    o_s = _sel_attn(h, sel, t)

    tqw = min(WIN_LEN, t)
    (o_w,) = _swa(h, h, h, n_r=A_KV_GROUPS, n_tiles=t // tqw, tq=tqw, nh=A_HPG, shared_kv=True,
                  max_dist=WIN_LEN - 1, q_map=lambda r: r, k_map=lambda r: CB_KW + r,
                  v_map=lambda r: CB_VW + r, out_cols=A_Q, o_map=lambda r: r, with_lse=False,
                  log2_scores=True,
                  name="nsa_window")

    ob_parts, lse_parts = [], []
    for gi, (window, dil) in enumerate(B_DILATIONS):
        if dil == 1:
            tqd = min(max(window, LANE), t)
            o, lse = _swa(h, h, h, n_r=1, n_tiles=t // tqd, tq=tqd, nh=B_HPG, shared_kv=False,
                          max_dist=window, q_map=lambda r, b=CB_QB // B_HPG + gi: b,
                          k_map=lambda r, b=CB_KB // B_HPG + gi: b,
                          v_map=lambda r, b=CB_VB // B_HPG + gi: b,
                          out_cols=B_HPG * LANE, o_map=lambda r: r, with_lse=True, log2_scores=False,
                          name="dilated_1")
        else:
            o, lse = _band_attn(h, t, q_cb=CB_QB + gi * B_HPG, k_cb=CB_KB + gi * B_HPG,
                                v_cb=CB_VB + gi * B_HPG, nh=B_HPG, window=window, dil=dil)
        ob_parts.append(o)
        lse_parts.append(lse)
    o_b = jnp.concatenate(ob_parts, axis=1)
    lse_b = jnp.concatenate(lse_parts, axis=1)
    mix_in = _mixprep(o_c, o_s, o_w, h, o_b, lse_b, t)
    return _mm(mix_in, w_out, out_dtype=F32, b_layer=e, name="even_out_proj")


def _odd_mixer(xb, w_in, w_out, o_idx, lq1, lk1, lq2, lk2, sub_g, layer, tabs):
    t, d = xb.shape
    per = MM_TN // LANE
    tile_modes = np.zeros(((2 * C_QK + C_V) // MM_TN,), np.int32)
    tile_modes[:C_QK // MM_TN] = 3
    tile_modes[C_QK // MM_TN:2 * C_QK // MM_TN] = 1
    h = _mm(xb, w_in, rope_tabs=tabs["qk"], rope_modes=jnp.asarray(tile_modes), b_layer=o_idx,
            name="odd_in_proj")
    o = _diff_attn(h, lq1, lk1, lq2, lk2, sub_g, layer, t)
    return _mm(o, w_out, out_dtype=F32, b_layer=o_idx, name="odd_out_proj")


def kernel(x, even_w_in, even_w_out, cmpk_pe, cmpk_w1, cmpk_b1, cmpk_w2, cmpk_b2, cmpv_pe, cmpv_w1, cmpv_b1, cmpv_w2, cmpv_b2, odd_w_in, odd_w_out, lam_q1, lam_k1, lam_q2, lam_k2, subln_g, ln_mix_g, ln_mix_b, ln_ffn_g, ln_ffn_b, router_w, router_b, exp_w_gu, exp_b_gu, exp_w_down, exp_b_down):
    bsz, t, d = x.shape
    assert bsz == 1
    xf = x.reshape(t, d)
    xb = xf.astype(BF16)
    pos = jnp.arange(t, dtype=jnp.int32)
    cmp_end = jnp.arange(t // CMP_STRIDE, dtype=jnp.int32) * CMP_STRIDE + (CMP_LEN - 1)
    tabs = {
        "qk": jnp.stack([_rope_tables(pos, 1.0), _rope_tables(pos, HEAD_DIM ** -0.5),
                         _rope_tables(pos, HEAD_DIM ** -0.5 * math.log2(math.e))]),
        "cmp": _rope_tables(cmp_end, 1.0),
    }
    even_w_out_b = even_w_out.astype(BF16)
    odd_w_in_b = odd_w_in.astype(BF16)
    odd_w_out_b = odd_w_out.astype(BF16)
    wgu_b = exp_w_gu.astype(BF16)
    wd_b = exp_w_down.astype(BF16)
    bgu = exp_b_gu[:, :, None, :].astype(F32)
    bd = exp_b_down[:, :, None, :].astype(F32)
    for layer in range(DEPTH):
        if layer % 2 == 0:
            e = layer // 2
            mix = _even_mixer(xb, even_w_in, even_w_out_b, e,
                              (cmpk_pe[e], cmpk_w1[e], cmpk_b1[e], cmpk_w2[e], cmpk_b2[e]),
                              (cmpv_pe[e], cmpv_w1[e], cmpv_b1[e], cmpv_w2[e], cmpv_b2[e]), tabs)
        else:
            o = layer // 2
            mix = _odd_mixer(xb, odd_w_in_b, odd_w_out_b, o, lam_q1[o], lam_k1[o], lam_q2[o],
                             lam_k2[o], subln_g[o], layer, tabs)
        xf, xb = _ln_res(xf, mix, ln_mix_g[layer], ln_mix_b[layer])
        xf, xb = _moe_ln(xf, router_w[layer], router_b[layer], wgu_b, bgu, wd_b, bd,
                         ln_ffn_g[layer], ln_ffn_b[layer], layer)
    return xf.reshape(bsz, t, d)
```

```python
import functools
import math

import jax
import jax.numpy as jnp
import numpy as np
from jax import lax
from jax.experimental import pallas as pl
from jax.experimental.pallas import tpu as pltpu

F32 = jnp.float32
BF16 = jnp.bfloat16

DEPTH = 4
HEAD_DIM = 128
ROPE_THETA = 500000.0
ROT_DIM = HEAD_DIM // 4
ROT_HALF = ROT_DIM // 2
NEG_INF = -1e30
SEL_FORCE = 1e9

A_HEADS = 20
A_KV_GROUPS = 4
A_HPG = A_HEADS // A_KV_GROUPS
CMP_LEN = 32
CMP_STRIDE = 16
CMP_HIDDEN = 4 * HEAD_DIM
SLC_LEN = 64
SLC_TOPK = 16
WIN_LEN = 512

B_DILATIONS = ((128, 1), (512, 4), (2048, 16))
B_HPG = 4
B_HEADS = B_HPG * len(B_DILATIONS)

C_HEADS = 16
C_VDIM = 2 * HEAD_DIM

N_EXPERTS = 32
TOP_K = 4
D_EXPERT = 384
SWIGLU_LIMIT = 7.0
SWIGLU_ALPHA = 1.702

DEEPNORM_ALPHA = (2.0 * DEPTH) ** 0.25

A_Q = A_HEADS * HEAD_DIM
A_KV = A_KV_GROUPS * HEAD_DIM
A_GATE = A_HEADS * 3
B_W = B_HEADS * HEAD_DIM
C_QK = 2 * C_HEADS * HEAD_DIM
C_V = C_HEADS * C_VDIM

LANE = 128
VMEM_LIMIT = 56 * 1024 * 1024

GATE_PAD = 512
EVEN_COLS = A_Q + 6 * A_KV + 3 * B_W + GATE_PAD
CB_QA = 0
CB_KC = A_Q // LANE
CB_VC = CB_KC + A_KV // LANE
CB_KS = CB_VC + A_KV // LANE
CB_VS = CB_KS + A_KV // LANE
CB_KW = CB_VS + A_KV // LANE
CB_VW = CB_KW + A_KV // LANE
CB_QB = CB_VW + A_KV // LANE
CB_KB = CB_QB + B_W // LANE
CB_VB = CB_KB + B_W // LANE
CB_GATE = CB_VB + B_W // LANE

MM_TN = 512


def _cparams(sem):
    return pltpu.CompilerParams(dimension_semantics=sem, vmem_limit_bytes=VMEM_LIMIT)


def _nt_dot(a, b):
    return lax.dot_general(a, b, (((1,), (1,)), ((), ())), preferred_element_type=F32)


def _rope_lanes(x, tab):
    c = tab[:, 0:LANE]
    sa = tab[:, LANE:2 * LANE]
    sb = tab[:, 2 * LANE:3 * LANE]
    return (x * c + pltpu.roll(x, LANE - ROT_HALF, axis=1) * sa
            + pltpu.roll(x, ROT_HALF, axis=1) * sb)


def _mm_kernel(*refs, has_bias, act, has_rope, n_sub):
    if has_rope:
        modes_ref, a_ref, b_ref = refs[0], refs[1], refs[2]
        rest = refs[3:]
    else:
        a_ref, b_ref = refs[0], refs[1]
        rest = refs[2:]
    idx = 0
    if has_bias:
        bias_ref = rest[idx]
        idx += 1
    if has_rope:
        tab_ref = rest[idx]
        idx += 1
    o_ref = rest[idx]

    acc = jnp.dot(a_ref[...], b_ref[...], preferred_element_type=F32)
    if has_bias:
        acc = acc + bias_ref[...]
    if act == "gelu":
        acc = jax.nn.gelu(acc, approximate=True)
    if not has_rope:
        o_ref[...] = acc.astype(o_ref.dtype)
        return
    mode = modes_ref[pl.program_id(1)]

    @pl.when(mode == 0)
    def _():
        o_ref[...] = acc.astype(o_ref.dtype)

    @pl.when(mode != 0)
    def _():
        tab = tab_ref[...]
        for s in range(n_sub):
            sl = slice(s * LANE, (s + 1) * LANE)
            o_ref[:, sl] = _rope_lanes(acc[:, sl], tab).astype(o_ref.dtype)


def _mm(a, b, *, bias=None, act=None, rope_tabs=None, rope_modes=None,
        out_dtype=None, tm=1024, tn=MM_TN, b_layer=None, name="mm"):
    out_dtype = BF16 if out_dtype is None else out_dtype
    m, k = a.shape
    k2, n = b.shape[-2:]
    assert k == k2 and (b.ndim == 3) == (b_layer is not None)
    tm = min(tm, m)
    tn = min(tn, n)
    assert m % tm == 0 and n % tn == 0
    has_bias = bias is not None
    has_rope = rope_tabs is not None
    grid = (m // tm, n // tn)
    kern = functools.partial(_mm_kernel, has_bias=has_bias, act=act, has_rope=has_rope,
                             n_sub=tn // LANE)
    if b_layer is None:
        b_spec = pl.BlockSpec((k, tn), lambda i, j, *_: (0, j))
    else:
        b_spec = pl.BlockSpec((None, k, tn), lambda i, j, *_: (b_layer, 0, j))
    if has_rope:
        in_specs = [pl.BlockSpec((tm, k), lambda i, j, md: (i, 0)), b_spec]
        args = [a, b]
        if has_bias:
            in_specs.append(pl.BlockSpec((1, tn), lambda i, j, md: (0, j)))
            args.append(bias.reshape(1, n).astype(F32))
        in_specs.append(pl.BlockSpec((None, tm, 3 * LANE),
                                     lambda i, j, md: (jnp.maximum(md[j] - 1, 0), i, 0)))
        args.append(rope_tabs)
        gs = pltpu.PrefetchScalarGridSpec(
            num_scalar_prefetch=1, grid=grid, in_specs=in_specs,
            out_specs=pl.BlockSpec((tm, tn), lambda i, j, md: (i, j)))
        return pl.pallas_call(kern, grid_spec=gs,
                              out_shape=jax.ShapeDtypeStruct((m, n), out_dtype),
                              compiler_params=_cparams(("parallel", "arbitrary")),
                              name=name)(rope_modes, *args)
    in_specs = [pl.BlockSpec((tm, k), lambda i, j: (i, 0)), b_spec]
    args = [a, b]
    if has_bias:
        in_specs.append(pl.BlockSpec((1, tn), lambda i, j: (0, j)))
        args.append(bias.reshape(1, n).astype(F32))
    return pl.pallas_call(kern, grid=grid, in_specs=in_specs,
                          out_specs=pl.BlockSpec((tm, tn), lambda i, j: (i, j)),
                          out_shape=jax.ShapeDtypeStruct((m, n), out_dtype),
                          compiler_params=_cparams(("parallel", "arbitrary")),
                          name=name)(*args)


def _ln_kernel(x_ref, y_ref, g_ref, b_ref, of_ref, ob_ref):
    z = DEEPNORM_ALPHA * x_ref[...] + y_ref[...].astype(F32)
    mu = jnp.mean(z, axis=-1, keepdims=True)
    zc = z - mu
    var = jnp.mean(zc * zc, axis=-1, keepdims=True)
    out = zc * lax.rsqrt(var + 1e-5) * g_ref[...] + b_ref[...]
    of_ref[...] = out
    ob_ref[...] = out.astype(BF16)


def _ln_res(x, y, g, b, tm=256):
    t, d = x.shape
    tm = min(tm, t)
    row = pl.BlockSpec((tm, d), lambda i: (i, 0))
    vec = pl.BlockSpec((1, d), lambda i: (0, 0))
    return pl.pallas_call(
        _ln_kernel, grid=(t // tm,), in_specs=[row, row, vec, vec], out_specs=[row, row],
        out_shape=[jax.ShapeDtypeStruct((t, d), F32), jax.ShapeDtypeStruct((t, d), BF16)],
        compiler_params=_cparams(("parallel",)), name="ln_res",
    )(x, y, g.reshape(1, d), b.reshape(1, d))


def _rope_tables(pos, scale):
    inv = ROPE_THETA ** (-jnp.arange(ROT_HALF, dtype=F32) / ROT_HALF)
    ang = pos.astype(F32)[:, None] * inv[None, :]
    cos = jnp.cos(ang)
    sin = jnp.sin(ang)
    n = pos.shape[0]
    ones = jnp.ones((n, HEAD_DIM - ROT_DIM), F32)
    zer = jnp.zeros((n, HEAD_DIM - ROT_HALF), F32)
    c = jnp.concatenate([cos, cos, ones], axis=1)
    sa = jnp.concatenate([-sin, zer], axis=1)
    sb = jnp.concatenate([jnp.zeros((n, ROT_HALF), F32), sin,
                          jnp.zeros((n, HEAD_DIM - ROT_DIM), F32)], axis=1)
    return jnp.concatenate([c, sa, sb], axis=1) * scale


def _swa_kernel(q_ref, kp_ref, kc_ref, vp_ref, vc_ref, *out_refs, nh, shared_kv, max_dist,
                tq, with_lse, log2_scores):
    assert not (with_lse and log2_scores)
    o_ref = out_refs[0]
    i = pl.program_id(1)
    row = lax.broadcasted_iota(jnp.int32, (tq, 2 * tq), 0)
    col = lax.broadcasted_iota(jnp.int32, (tq, 2 * tq), 1)
    dist = row + tq - col
    mask = (dist >= 0) & (dist <= max_dist) & (col + (i - 1) * tq >= 0)
    for h in range(nh):
        kv = 0 if shared_kv else h
        hs = slice(h * LANE, (h + 1) * LANE)
        ks = slice(kv * LANE, (kv + 1) * LANE)
        q = q_ref[:, hs]
        k = jnp.concatenate([kp_ref[:, ks], kc_ref[:, ks]], axis=0)
        v = jnp.concatenate([vp_ref[:, ks], vc_ref[:, ks]], axis=0)
        s = jnp.where(mask, _nt_dot(q, k), NEG_INF)
        m = jnp.max(s, axis=-1, keepdims=True)
        e = jnp.exp2(s - m) if log2_scores else jnp.exp(s - m)
        den = jnp.sum(e, axis=-1, keepdims=True)
        o = jnp.dot(e.astype(BF16), v, preferred_element_type=F32) / den
        o_ref[:, hs] = o.astype(o_ref.dtype)
        if with_lse:
            out_refs[1][:, hs] = jnp.broadcast_to(m + jnp.log(den), (tq, LANE))


def _swa(qsrc, ksrc, vsrc, *, n_r, n_tiles, tq, nh, shared_kv, max_dist, q_map, k_map, v_map,
         out_cols, o_map, with_lse, log2_scores, name):
    kvw = LANE if shared_kv else nh * LANE
    qw = nh * LANE
    prev = lambda f: (lambda r, i: (jnp.maximum(i - 1, 0), f(r)))
    cur = lambda f: (lambda r, i: (i, f(r)))
    in_specs = [pl.BlockSpec((tq, qw), cur(q_map)),
                pl.BlockSpec((tq, kvw), prev(k_map)), pl.BlockSpec((tq, kvw), cur(k_map)),
                pl.BlockSpec((tq, kvw), prev(v_map)), pl.BlockSpec((tq, kvw), cur(v_map))]
    rows = n_tiles * tq
    out_shape = [jax.ShapeDtypeStruct((rows, out_cols), BF16)]
    out_specs = [pl.BlockSpec((tq, qw), cur(o_map))]
    if with_lse:
        out_shape.append(jax.ShapeDtypeStruct((rows, out_cols), F32))
        out_specs.append(pl.BlockSpec((tq, qw), cur(o_map)))
    kern = functools.partial(_swa_kernel, nh=nh, shared_kv=shared_kv, max_dist=max_dist, tq=tq,
                             with_lse=with_lse, log2_scores=log2_scores)
    return pl.pallas_call(kern, grid=(n_r, n_tiles), in_specs=in_specs, out_specs=out_specs,
                          out_shape=out_shape,
                          compiler_params=_cparams(("parallel", "arbitrary")),
                          name=name)(qsrc, ksrc, ksrc, vsrc, vsrc)


def _band_kernel(q_ref, k_ref, v_ref, o_ref, lse_ref, *, tq, band, window, dil):
    i = pl.program_id(1)
    start = pl.multiple_of(jnp.maximum((i + 1) * tq - band, 0), tq)
    k = k_ref[pl.ds(start, band), :]
    v = v_ref[pl.ds(start, band), :]
    dist = (lax.broadcasted_iota(jnp.int32, (tq, band), 0)
            - lax.broadcasted_iota(jnp.int32, (tq, band), 1)) + (i * tq - start)
    valid = (dist >= 0) & (dist <= window) & ((dist & (dil - 1)) == 0)
    s = jnp.where(valid, _nt_dot(q_ref[...], k), NEG_INF)
    m = jnp.max(s, axis=-1, keepdims=True)
    e = jnp.exp(s - m)
    l = jnp.sum(e, axis=-1, keepdims=True)
    o_ref[...] = (jnp.dot(e.astype(BF16), v, preferred_element_type=F32) / l).astype(o_ref.dtype)
    lse_ref[...] = jnp.broadcast_to(m + jnp.log(l), (tq, LANE))


def _band_attn(h, t, *, q_cb, k_cb, v_cb, nh, window, dil, tq=256):
    tq = min(tq, t)
    assert dil & (dil - 1) == 0 and window % tq == 0
    band = min(window + tq, t)
    kern = functools.partial(_band_kernel, tq=tq, band=band, window=window, dil=dil)
    blk = pl.BlockSpec((tq, LANE), lambda hd, i: (i, hd))
    return pl.pallas_call(
        kern, grid=(nh, t // tq),
        in_specs=[pl.BlockSpec((tq, LANE), lambda hd, i: (i, q_cb + hd)),
                  pl.BlockSpec((t, LANE), lambda hd, i: (0, k_cb + hd)),
                  pl.BlockSpec((t, LANE), lambda hd, i: (0, v_cb + hd))],
        out_specs=[blk, blk],
        out_shape=[jax.ShapeDtypeStruct((t, nh * LANE), BF16), jax.ShapeDtypeStruct((t, nh * LANE), F32)],
        compiler_params=_cparams(("parallel", "arbitrary")), name="dilated_band_%d" % dil,
    )(h, h, h)


def _cmp_kernel(q_ref, kc_ref, vc_ref, ov_ref, oc_ref, sel_ref, *, tq, n_cmp_pad, n_slc):
    i = pl.program_id(1)
    tqv = i * tq + lax.broadcasted_iota(jnp.int32, (tq, 1), 0)
    cmp_end = CMP_STRIDE * lax.broadcasted_iota(jnp.int32, (1, n_cmp_pad), 1) + (CMP_LEN - 1)
    maskc = cmp_end <= tqv
    kc = kc_ref[...]
    vc = vc_ref[...]
    psum = jnp.zeros((tq, n_cmp_pad), F32)
    for h in range(A_HPG):
        hs = slice(h * LANE, (h + 1) * LANE)
        s = jnp.where(maskc, _nt_dot(q_ref[:, hs], kc), NEG_INF)
        m = jnp.max(s, axis=-1, keepdims=True)
        e = jnp.where(maskc, jnp.exp2(s - m), 0.0)
        den = jnp.sum(e, axis=-1, keepdims=True)
        p = e / jnp.maximum(den, 1e-30)
        oc_ref[:, hs] = jnp.dot(p.astype(BF16), vc, preferred_element_type=F32).astype(oc_ref.dtype)
        psum = psum + p
    ov = ov_ref[...]
    p1 = psum.astype(BF16)
    r1 = psum - p1.astype(F32)
    p2 = r1.astype(BF16)
    p3 = (r1 - p2.astype(F32)).astype(BF16)
    imp = (jnp.dot(p1, ov, preferred_element_type=F32) + jnp.dot(p2, ov, preferred_element_type=F32)
           + jnp.dot(p3, ov, preferred_element_type=F32))
    j = lax.broadcasted_iota(jnp.int32, (1, n_slc), 1)
    jf = j.astype(F32)
    tb = tqv // SLC_LEN
    forced = (j == 0) | (j == tb) | (j == tb - 1)
    val = jnp.where(forced, SEL_FORCE, jnp.where(j <= tb, imp, -SEL_FORCE))
    sel = jnp.zeros((tq, n_slc), F32)
    for _ in range(min(SLC_TOPK, n_slc)):
        m = jnp.max(val, axis=-1, keepdims=True)
        first = jnp.min(jnp.where(val == m, jf, float(n_slc)), axis=-1, keepdims=True)
        hit = jf == first
        sel = jnp.where(hit & (m > -0.5 * SEL_FORCE), 1.0, sel)
        val = jnp.where(hit, -3e38, val)
    sel_ref[...] = sel.astype(sel_ref.dtype)


def _cmp_attn(h, kc, vc, t, tq=512):
    n_cmp_pad = kc.shape[0] // A_KV_GROUPS
    n_slc = t // SLC_LEN
    ci = np.arange(n_cmp_pad)[:, None] * CMP_STRIDE
    sj = np.arange(n_slc)[None, :] * SLC_LEN
    overlap = ((ci < sj + SLC_LEN) & (ci + CMP_LEN > sj) & (np.arange(n_cmp_pad)[:, None] < t // CMP_STRIDE - 1))
    overlap = jnp.asarray(overlap.astype(np.float32), BF16)
    kern = functools.partial(_cmp_kernel, tq=tq, n_cmp_pad=n_cmp_pad, n_slc=n_slc)
    qw = A_HPG * LANE
    return pl.pallas_call(
        kern, grid=(A_KV_GROUPS, t // tq),
        in_specs=[pl.BlockSpec((tq, qw), lambda g, i: (i, g)),
                  pl.BlockSpec((n_cmp_pad, LANE), lambda g, i: (g, 0)),
                  pl.BlockSpec((n_cmp_pad, LANE), lambda g, i: (g, 0)),
                  pl.BlockSpec((n_cmp_pad, n_slc), lambda g, i: (0, 0))],
        out_specs=[pl.BlockSpec((tq, qw), lambda g, i: (i, g)),
                   pl.BlockSpec((tq, n_slc), lambda g, i: (i, g))],
        out_shape=[jax.ShapeDtypeStruct((t, A_Q), BF16),
                   jax.ShapeDtypeStruct((t, A_KV_GROUPS * n_slc), BF16)],
        compiler_params=_cparams(("parallel", "arbitrary")), name="cmp_attn_topk",
    )(h, kc, vc, overlap)


ROW_CHUNK = 32


def _flash_pipeline(n, score_stage, softmax_stage, value_stage):
    @pl.when(n > 0)
    def _():
        score_stage(0, 0)

    def step(t, slot):
        softmax_stage(slot)
        value_stage(1 - slot, jnp.maximum(t - 1, 0))
        score_stage(1 - slot, jnp.minimum(t + 1, n - 1))

    def pair(u, c):
        step(2 * u, 0)
        step(2 * u + 1, 1)
        return c

    lax.fori_loop(0, n // 2, pair, 0)
    last = jnp.maximum(n - 1, 0)

    @pl.when(n % 2 == 1)
    def _():
        step(n - 1, 0)
        value_stage(0, last)

    @pl.when(n % 2 == 0)
    def _():
        value_stage(1, last)


def _sel_kernel(q_ref, k_ref, v_ref, sel_ref, o_ref, q5_sc, s_sc, bias_sc, p_sc, acc_sc, m_sc, l_sc,
                alpha_sc, *, tq, tk, n_slc):
    i = pl.program_id(1)
    nh = A_HPG
    for h in range(nh):
        q5_sc[h * tq:(h + 1) * tq, :] = q_ref[:, h * LANE:(h + 1) * LANE]
    m_sc[...] = jnp.full(m_sc.shape, NEG_INF, F32)
    l_sc[...] = jnp.zeros_like(l_sc)
    acc_sc[...] = jnp.zeros_like(acc_sc)
    p_sc[1] = jnp.zeros(p_sc.shape[1:], p_sc.dtype)
    alpha_sc[1] = jnp.ones(alpha_sc.shape[1:], F32)
    selb = sel_ref[...]
    tqv = i * tq + lax.broadcasted_iota(jnp.int32, (tq, 1), 0)
    blk_per_tile = tk // SLC_LEN
    blk_gap = (lax.broadcasted_iota(jnp.int32, (n_slc, tk), 0)
               - lax.broadcasted_iota(jnp.int32, (n_slc, tk), 1) // SLC_LEN)
    kcol = lax.broadcasted_iota(jnp.int32, (1, tk), 1)
    n_kv = (i * tq + tq + tk - 1) // tk
    nrep = tk // LANE

    def score_stage(slot, t):
        k = k_ref[pl.ds(pl.multiple_of(t * tk, tk), tk), :]
        s_sc[slot] = _nt_dot(q5_sc[...], k)
        expand = jnp.where(blk_gap == t * blk_per_tile, 1.0, 0.0).astype(BF16)
        picked = jnp.dot(selb, expand, preferred_element_type=F32)
        bias_sc[slot] = jnp.where((picked > 0.5) & (kcol + t * tk <= tqv), 0.0, NEG_INF)

    def softmax_stage(slot):
        for c in range(nh * tq // ROW_CHUNK):
            rows = slice(c * ROW_CHUNK, (c + 1) * ROW_CHUNK)
            r0 = (c * ROW_CHUNK) % tq
            s = s_sc[slot, rows, :] + bias_sc[slot, r0:r0 + ROW_CHUNK, :]
            m_old = m_sc[rows, :]
            m_new = jnp.maximum(m_old, jnp.max(s, axis=-1, keepdims=True))
            alpha = jnp.exp2(m_old - m_new)
            p = jnp.exp2(s - jnp.concatenate([m_new] * nrep, axis=1))
            l_sc[rows, :] = alpha * l_sc[rows, :] + jnp.sum(p, axis=-1, keepdims=True)
            m_sc[rows, :] = m_new
            alpha_sc[slot, rows, :] = alpha
            p_sc[slot, rows, :] = p.astype(p_sc.dtype)

    def value_stage(slot, t):
        v = v_ref[pl.ds(pl.multiple_of(t * tk, tk), tk), :]
        acc_sc[...] = acc_sc[...] * alpha_sc[slot] + jnp.dot(p_sc[slot], v, preferred_element_type=F32)

    _flash_pipeline(n_kv, score_stage, softmax_stage, value_stage)
    o = acc_sc[...] / l_sc[...]
    for h in range(nh):
        o_ref[:, h * LANE:(h + 1) * LANE] = o[h * tq:(h + 1) * tq].astype(o_ref.dtype)


def _sel_attn(h, sel, t, tq=128, tk=512):
    n_slc = t // SLC_LEN
    tk = min(tk, t)
    qw = A_HPG * LANE
    rows = A_HPG * tq
    kern = functools.partial(_sel_kernel, tq=tq, tk=tk, n_slc=n_slc)
    return pl.pallas_call(
        kern, grid=(A_KV_GROUPS, t // tq),
        in_specs=[pl.BlockSpec((tq, qw), lambda g, i: (i, g)),
                  pl.BlockSpec((t, LANE), lambda g, i: (0, CB_KS + g)),
                  pl.BlockSpec((t, LANE), lambda g, i: (0, CB_VS + g)),
                  pl.BlockSpec((tq, n_slc), lambda g, i: (i, g))],
        out_specs=pl.BlockSpec((tq, qw), lambda g, i: (i, g)),
        out_shape=jax.ShapeDtypeStruct((t, A_Q), BF16),
        scratch_shapes=[pltpu.VMEM((rows, LANE), BF16), pltpu.VMEM((2, rows, tk), F32),
                        pltpu.VMEM((2, tq, tk), F32), pltpu.VMEM((2, rows, tk), BF16),
                        pltpu.VMEM((rows, LANE), F32), pltpu.VMEM((rows, LANE), F32),
                        pltpu.VMEM((rows, LANE), F32), pltpu.VMEM((2, rows, LANE), F32)],
        compiler_params=_cparams(("parallel", "arbitrary")), name="sel_attn",
    )(h, h, h, sel)


def _mixprep_kernel(oc_ref, os_ref, ow_ref, gate_ref, ob_ref, lse_ref, out_ref):
    gate = jax.nn.sigmoid(gate_ref[...].astype(F32))
    for h in range(A_HEADS):
        hs = slice(h * LANE, (h + 1) * LANE)
        o = (gate[:, 3 * h:3 * h + 1] * oc_ref[:, hs].astype(F32)
             + gate[:, 3 * h + 1:3 * h + 2] * os_ref[:, hs].astype(F32)
             + gate[:, 3 * h + 2:3 * h + 3] * ow_ref[:, hs].astype(F32))
        out_ref[:, hs] = o.astype(out_ref.dtype)
    ng = len(B_DILATIONS)
    for hi in range(B_HPG):
        lses = [lse_ref[:, (g * B_HPG + hi) * LANE:(g * B_HPG + hi + 1) * LANE] for g in range(ng)]
        m = functools.reduce(jnp.maximum, lses)
        es = [jnp.exp(x - m) for x in lses]
        den = functools.reduce(lambda a, b: a + b, es)
        for g in range(ng):
            src = slice((g * B_HPG + hi) * LANE, (g * B_HPG + hi + 1) * LANE)
            dst = slice(A_Q + (g * B_HPG + hi) * LANE, A_Q + (g * B_HPG + hi + 1) * LANE)
            out_ref[:, dst] = (ob_ref[:, src].astype(F32) * (es[g] / den)).astype(out_ref.dtype)


def _mixprep(o_c, o_s, o_w, h, o_b, lse_b, t, tq=256):
    tq = min(tq, t)
    a = pl.BlockSpec((tq, A_Q), lambda i: (i, 0))
    b = pl.BlockSpec((tq, B_W), lambda i: (i, 0))
    return pl.pallas_call(
        _mixprep_kernel, grid=(t // tq,),
        in_specs=[a, a, a, pl.BlockSpec((tq, LANE), lambda i: (i, CB_GATE)), b, b],
        out_specs=pl.BlockSpec((tq, A_Q + B_W), lambda i: (i, 0)),
        out_shape=jax.ShapeDtypeStruct((t, A_Q + B_W), BF16),
        compiler_params=_cparams(("parallel",)), name="even_mixprep",
    )(o_c, o_s, o_w, h, o_b, lse_b)


def _diff_kernel(q_ref, k_ref, v_ref, lq1_ref, lk1_ref, lq2_ref, lk2_ref, g_ref, o_ref,
                 s_sc, p_sc, acc_sc, m_sc, l_sc, alpha_sc, *, tq, tk, lam_init):
    i = pl.program_id(1)
    q1 = q_ref[:, 0:LANE]
    q2 = q_ref[:, LANE:2 * LANE]
    lam = (jnp.exp(jnp.sum(lq1_ref[...] * lk1_ref[...], axis=-1, keepdims=True))
           - jnp.exp(jnp.sum(lq2_ref[...] * lk2_ref[...], axis=-1, keepdims=True)) + lam_init)

    m_sc[...] = jnp.full(m_sc.shape, NEG_INF, F32)
    l_sc[...] = jnp.zeros_like(l_sc)
    acc_sc[...] = jnp.zeros_like(acc_sc)
    p_sc[1] = jnp.zeros(p_sc.shape[1:], p_sc.dtype)
    alpha_sc[1] = jnp.ones(alpha_sc.shape[1:], F32)
    nrep = tk // LANE
    rep = C_VDIM // LANE

    def score_stage(slot, start):
        k = k_ref[pl.ds(start, tk), :]
        s_sc[slot, 0] = _nt_dot(q1, k[:, 0:LANE])
        s_sc[slot, 1] = _nt_dot(q2, k[:, LANE:2 * LANE])

    def softmax_stage(slot, diag_off):
        for hd in range(2):
            for c in range(tq // ROW_CHUNK):
                rows = slice(c * ROW_CHUNK, (c + 1) * ROW_CHUNK)
                s = s_sc[slot, hd, rows, :]
                if diag_off is not None:
                    row = lax.broadcasted_iota(jnp.int32, (ROW_CHUNK, tk), 0) + c * ROW_CHUNK
                    col = lax.broadcasted_iota(jnp.int32, (ROW_CHUNK, tk), 1) + diag_off
                    s = jnp.where(col <= row, s, NEG_INF)
                m_old = m_sc[hd, rows, :]
                m_new = jnp.maximum(m_old, jnp.max(s, axis=-1, keepdims=True))
                alpha = jnp.exp2(m_old - m_new)
                p = jnp.exp2(s - jnp.concatenate([m_new] * nrep, axis=1))
                l_sc[hd, rows, :] = alpha * l_sc[hd, rows, :] + jnp.sum(p, axis=-1, keepdims=True)
                m_sc[hd, rows, :] = m_new
                alpha_sc[slot, hd, rows, :] = alpha
                p_sc[slot, hd, rows, :] = p.astype(p_sc.dtype)

    def value_stage(slot, start):
        v = v_ref[pl.ds(start, tk), :]
        for hd in range(2):
            pv = jnp.dot(p_sc[slot, hd], v, preferred_element_type=F32)
            alpha = alpha_sc[slot, hd]
            acc_sc[hd] = acc_sc[hd] * jnp.concatenate([alpha] * rep, axis=1) + pv

    per = tq // tk
    n_full = i * per
    _flash_pipeline(n_full, lambda slot, t: score_stage(slot, pl.multiple_of(t * tk, tk)),
                    lambda slot: softmax_stage(slot, None),
                    lambda slot, t: value_stage(slot, pl.multiple_of(t * tk, tk)))
    for u in range(per):
        start = pl.multiple_of(i * tq + u * tk, tk)
        score_stage(0, start)
        softmax_stage(0, u * tk)
        value_stage(0, start)
    l1 = jnp.concatenate([l_sc[0]] * rep, axis=1)
    l2 = jnp.concatenate([l_sc[1]] * rep, axis=1)
    o = acc_sc[0] / l1 - lam * (acc_sc[1] / l2)
    o = o * lax.rsqrt(jnp.mean(o * o, axis=-1, keepdims=True) + 1e-5) * g_ref[...] * (1.0 - lam_init)
    o_ref[...] = o.astype(o_ref.dtype)


def _diff_attn(h, lq1, lk1, lq2, lk2, sub_g, layer, t, tq=512, tk=512):
    tq = min(tq, t)
    tk = min(tk, tq)
    lam_init = 0.8 - 0.6 * math.exp(-0.3 * layer)
    kern = functools.partial(_diff_kernel, tq=tq, tk=tk, lam_init=lam_init)
    vec = pl.BlockSpec((1, LANE), lambda hh, i: (0, 0))
    qkb = C_QK // C_VDIM
    return pl.pallas_call(
        kern, grid=(C_HEADS, t // tq),
        in_specs=[pl.BlockSpec((tq, C_VDIM), lambda hh, i: (i, hh)),
                  pl.BlockSpec((t, C_VDIM), lambda hh, i: (0, qkb + hh)),
                  pl.BlockSpec((t, C_VDIM), lambda hh, i: (0, 2 * qkb + hh)),
                  vec, vec, vec, vec, pl.BlockSpec((1, C_VDIM), lambda hh, i: (0, 0))],
        out_specs=pl.BlockSpec((tq, C_VDIM), lambda hh, i: (i, hh)),
        out_shape=jax.ShapeDtypeStruct((t, C_V), BF16),
        scratch_shapes=[pltpu.VMEM((2, 2, tq, tk), F32), pltpu.VMEM((2, 2, tq, tk), BF16),
                        pltpu.VMEM((2, tq, C_VDIM), F32), pltpu.VMEM((2, tq, LANE), F32),
                        pltpu.VMEM((2, tq, LANE), F32), pltpu.VMEM((2, 2, tq, LANE), F32)],
        compiler_params=_cparams(("parallel", "arbitrary")), name="diff_attn",
    )(h, h, h, lq1.reshape(1, LANE).astype(F32), lk1.reshape(1, LANE).astype(F32),
      lq2.reshape(1, LANE).astype(F32), lk2.reshape(1, LANE).astype(F32),
      sub_g.reshape(1, C_VDIM).astype(F32))


ROUTE_EID = 0
ROUTE_RANK = 4
ROUTE_W = 8
MOE_TM = 256


def _router_kernel(x_ref, whi_ref, wlo_ref, b_ref, route_ref, cnt_ref, carry_ref):
    i = pl.program_id(0)

    @pl.when(i == 0)
    def _():
        carry_ref[...] = jnp.zeros_like(carry_ref)

    x = x_ref[...]
    x_hi = x.astype(BF16)
    x_lo = (x - x_hi.astype(F32)).astype(BF16)
    logits = (jnp.dot(x_hi, whi_ref[...], preferred_element_type=F32)
              + jnp.dot(x_lo, whi_ref[...], preferred_element_type=F32)
              + jnp.dot(x_hi, wlo_ref[...], preferred_element_type=F32)) + b_ref[...]
    tm, ne = logits.shape
    lane = lax.broadcasted_iota(jnp.int32, (1, ne), 1)
    jf = lane.astype(F32)
    val = logits
    tops, hits, firsts = [], [], []
    for _ in range(TOP_K):
        m = jnp.max(val, axis=-1, keepdims=True)
        first = jnp.min(jnp.where(val == m, jf, float(ne)), axis=-1, keepdims=True)
        hit = jf == first
        tops.append(m)
        hits.append(hit)
        firsts.append(first)
        val = jnp.where(hit, -3e38, val)
    es = [jnp.exp(x - tops[0]) for x in tops]
    den = functools.reduce(lambda a, b: a + b, es)
    picked = jnp.zeros((tm, ne), F32)
    for hit in hits:
        picked = jnp.where(hit, 1.0, picked)
    row = lax.broadcasted_iota(jnp.int32, (tm, tm), 0)
    col = lax.broadcasted_iota(jnp.int32, (tm, tm), 1)
    tri = jnp.where(row > col, 1.0, 0.0).astype(BF16)
    rank = jnp.dot(tri, picked.astype(BF16), preferred_element_type=F32) + carry_ref[...]
    carry_ref[...] += jnp.sum(picked, axis=0, keepdims=True)
    cnt_ref[...] = carry_ref[...]
    route = jnp.zeros((tm, ne), F32)
    for k in range(TOP_K):
        rk = jnp.sum(jnp.where(hits[k], rank, 0.0), axis=-1, keepdims=True)
        route = jnp.where(lane == ROUTE_EID + k, firsts[k], route)
        route = jnp.where(lane == ROUTE_RANK + k, rk, route)
        route = jnp.where(lane == ROUTE_W + k, es[k] / den, route)
    route_ref[...] = route


def _router(x, rw, rb, tm=256):
    t, d = x.shape
    tm = min(tm, t)
    rw_p = jnp.pad(rw.astype(F32), ((0, 0), (0, LANE - N_EXPERTS)))
    rb_p = jnp.pad(rb.astype(F32), (0, LANE - N_EXPERTS), constant_values=NEG_INF).reshape(1, LANE)
    w_hi = rw_p.astype(BF16)
    w_lo = (rw_p - w_hi.astype(F32)).astype(BF16)
    wblk = pl.BlockSpec((d, LANE), lambda i: (0, 0))
    return pl.pallas_call(
        _router_kernel, grid=(t // tm,),
        in_specs=[pl.BlockSpec((tm, d), lambda i: (i, 0)), wblk, wblk,
                  pl.BlockSpec((1, LANE), lambda i: (0, 0))],
        out_specs=[pl.BlockSpec((tm, LANE), lambda i: (i, 0)), pl.BlockSpec((1, LANE), lambda i: (0, 0))],
        out_shape=[jax.ShapeDtypeStruct((t, LANE), F32), jax.ShapeDtypeStruct((1, LANE), F32)],
        scratch_shapes=[pltpu.VMEM((1, LANE), F32)],
        compiler_params=_cparams(("arbitrary",)), name="moe_router",
    )(x, w_hi, w_lo, rb_p)


def _row_copy(src, src_row, dst, dst_row, sem):
    return pltpu.make_async_copy(src.at[pl.ds(src_row, 1)], dst.at[pl.ds(dst_row, 1)], sem)


def _dispatch_kernel(gend_ref, dest_ref, x_ref, xs_hbm, zeros_ref, sem, *, tm):
    i = pl.program_id(0)

    @pl.when(i == 0)
    def _():
        zeros_ref[...] = jnp.zeros_like(zeros_ref)

        def clear(e):
            start = pl.multiple_of(gend_ref[e] - MOE_TM, MOE_TM)
            return pltpu.make_async_copy(zeros_ref, xs_hbm.at[pl.ds(start, MOE_TM)], sem)
        for e in range(N_EXPERTS):
            clear(e).start()
        for e in range(N_EXPERTS):
            clear(e).wait()

        def clear_tail(j, c):
            cp = pltpu.make_async_copy(
                zeros_ref, xs_hbm.at[pl.ds(pl.multiple_of(j * MOE_TM, MOE_TM), MOE_TM)], sem)
            cp.start()
            cp.wait()
            return c

        lax.fori_loop(gend_ref[N_EXPERTS - 1] // MOE_TM, xs_hbm.shape[0] // MOE_TM, clear_tail, 0)

    def issue(r, c):
        for k in range(TOP_K):
            _row_copy(x_ref, r, xs_hbm, dest_ref[TOP_K * r + k], sem).start()
        return c

    lax.fori_loop(0, tm, issue, 0)

    def drain(r, c):
        for k in range(TOP_K):
            _row_copy(x_ref, 0, xs_hbm, 0, sem).wait()
        return c

    lax.fori_loop(0, tm, drain, 0)


def _dispatch(xf, dest_flat, gend, n_rows, tm=256):
    t, d = xf.shape
    tm = min(tm, t)
    gs = pltpu.PrefetchScalarGridSpec(
        num_scalar_prefetch=1, grid=(t // tm,),
        in_specs=[pl.BlockSpec((TOP_K * tm,), lambda i, ge: (i,), memory_space=pltpu.SMEM),
                  pl.BlockSpec((tm, d), lambda i, ge: (i, 0))],
        out_specs=pl.BlockSpec(memory_space=pl.ANY),
        scratch_shapes=[pltpu.VMEM((MOE_TM, d), F32), pltpu.SemaphoreType.DMA(())])
    return pl.pallas_call(
        functools.partial(_dispatch_kernel, tm=tm), grid_spec=gs, out_shape=jax.ShapeDtypeStruct((n_rows, d), F32),
        compiler_params=_cparams(("arbitrary",)), name="moe_dispatch",
    )(gend, dest_flat, xf)


def _expert_kernel(eid_ref, nused_ref, xs_ref, wgu_ref, bgu_ref, wd_ref, bd_ref, pick_ref, ys_ref):
    j = pl.program_id(0)

    @pl.when(j < nused_ref[0])
    def _():
        x = xs_ref[...].astype(BF16)
        hgu = jnp.dot(x, wgu_ref[...], preferred_element_type=F32) + bgu_ref[...]
        glu = jnp.minimum(hgu, SWIGLU_LIMIT)
        gated = glu * jax.nn.sigmoid(SWIGLU_ALPHA * glu)
        lin = jnp.clip(hgu, -SWIGLU_LIMIT, SWIGLU_LIMIT) + 1.0
        prod = jnp.concatenate(
            [gated[:, s:s + LANE] * pltpu.roll(lin[:, s:s + LANE], LANE - 1, axis=1)
             for s in range(0, 2 * D_EXPERT, LANE)], axis=1)
        act = jnp.dot(prod.astype(BF16), pick_ref[...], preferred_element_type=F32)
        ys_ref[...] = jnp.dot(act.astype(BF16), wd_ref[...], preferred_element_type=F32) + bd_ref[...]

    @pl.when(j >= nused_ref[0])
    def _():
        ys_ref[...] = jnp.zeros_like(ys_ref)


def _experts(xs, tile_eid, nused, wgu, bgu, wd, bd, layer):
    n_rows, d = xs.shape
    n_tiles = n_rows // MOE_TM
    pick = np.zeros((2 * D_EXPERT, D_EXPERT), np.float32)
    pick[2 * np.arange(D_EXPERT), np.arange(D_EXPERT)] = 1.0
    tile = lambda j, eid, nu: (jnp.minimum(j, nu[0] - 1), 0)
    per_expert = lambda j, eid, nu: (layer, eid[j], 0, 0)
    gs = pltpu.PrefetchScalarGridSpec(
        num_scalar_prefetch=2, grid=(n_tiles,),
        in_specs=[pl.BlockSpec((MOE_TM, d), tile),
                  pl.BlockSpec((None, None, d, 2 * D_EXPERT), per_expert),
                  pl.BlockSpec((None, None, 1, 2 * D_EXPERT), per_expert),
                  pl.BlockSpec((None, None, D_EXPERT, d), per_expert),
                  pl.BlockSpec((None, None, 1, d), per_expert),
                  pl.BlockSpec((2 * D_EXPERT, D_EXPERT), lambda j, eid, nu: (0, 0))],
        out_specs=pl.BlockSpec((MOE_TM, d), lambda j, eid, nu: (j, 0)))
    return pl.pallas_call(
        _expert_kernel, grid_spec=gs, out_shape=jax.ShapeDtypeStruct((n_rows, d), F32),
        compiler_params=_cparams(("arbitrary",)), name="moe_experts",
    )(tile_eid, nused, xs, wgu, bgu, wd, bd, jnp.asarray(pick, BF16))


def _combine_ln_kernel(dest_cur_ref, dest_nxt_ref, x_ref, route_ref, g_ref, b_ref, ys_hbm,
                       of_ref, ob_ref, ybuf, sems, *, tm):
    i = pl.program_id(0)
    slot = i % 2

    per_row = LANE // TOP_K

    def gather(dest_ref, s):
        for q in range(tm // per_row):
            def issue(rr, c, q=q):
                for k in range(TOP_K):
                    src_row = dest_ref[q, TOP_K * rr + k]
                    _row_copy(ys_hbm, src_row, ybuf.at[s, k], q * per_row + rr, sems.at[s]).start()
                return c
            lax.fori_loop(0, per_row, issue, 0)

    @pl.when(i == 0)
    def _():
        gather(dest_cur_ref, slot)

    @pl.when(i + 1 < pl.num_programs(0))
    def _():
        gather(dest_nxt_ref, 1 - slot)

    def drain(r, c):
        for k in range(TOP_K):
            _row_copy(ys_hbm, 0, ybuf.at[slot, k], 0, sems.at[slot]).wait()
        return c

    lax.fori_loop(0, tm, drain, 0)
    route = route_ref[...]
    z = DEEPNORM_ALPHA * x_ref[...]
    for k in range(TOP_K):
        z = z + route[:, ROUTE_W + k:ROUTE_W + k + 1] * ybuf[slot, k]
    mu = jnp.mean(z, axis=-1, keepdims=True)
    zc = z - mu
    var = jnp.mean(zc * zc, axis=-1, keepdims=True)
    out = zc * lax.rsqrt(var + 1e-5) * g_ref[...] + b_ref[...]
    of_ref[...] = out
    ob_ref[...] = out.astype(BF16)


def _combine_ln(xf, ys, dest_flat, route, g, b, tm=128):
    t, d = xf.shape
    tm = min(tm, t)
    n = t // tm
    row = pl.BlockSpec((tm, d), lambda i: (i, 0))
    lanes = pl.BlockSpec((tm, LANE), lambda i: (i, 0))
    vec = pl.BlockSpec((1, d), lambda i: (0, 0))
    kern = functools.partial(_combine_ln_kernel, tm=tm)
    idx_rows = TOP_K * tm // LANE
    dest3 = dest_flat.reshape(n, idx_rows, LANE)
    idx_blk = lambda f: pl.BlockSpec((None, idx_rows, LANE), f, memory_space=pltpu.SMEM)
    return pl.pallas_call(
        kern, grid=(n,),
        in_specs=[idx_blk(lambda i: (i, 0, 0)), idx_blk(lambda i: (jnp.minimum(i + 1, n - 1), 0, 0)),
                  row, lanes, vec, vec, pl.BlockSpec(memory_space=pl.ANY)],
        out_specs=[row, row],
        out_shape=[jax.ShapeDtypeStruct((t, d), F32), jax.ShapeDtypeStruct((t, d), BF16)],
        scratch_shapes=[pltpu.VMEM((2, TOP_K, tm, d), F32), pltpu.SemaphoreType.DMA((2,))],
        compiler_params=_cparams(("arbitrary",)), name="moe_combine_ln",
    )(dest3, dest3, xf, route, g.reshape(1, d), b.reshape(1, d), ys)


def _moe_ln(xf, rw, rb, wgu, bgu, wd, bd, g, b, layer):
    t, d = xf.shape
    route, cnt = _router(xf, rw, rb)
    cnt = cnt[0, :N_EXPERTS].astype(jnp.int32)
    tiles_e = jnp.maximum((cnt + MOE_TM - 1) // MOE_TM, 1)
    gend = jnp.cumsum(tiles_e) * MOE_TM
    gstart = gend - tiles_e * MOE_TM
    eid = route[:, ROUTE_EID:ROUTE_EID + TOP_K].astype(jnp.int32)
    rank = route[:, ROUTE_RANK:ROUTE_RANK + TOP_K].astype(jnp.int32)
    onehot = eid[..., None] == jnp.arange(N_EXPERTS, dtype=jnp.int32)
    dest = jnp.sum(jnp.where(onehot, gstart, 0), axis=-1) + rank
    dest_flat = dest.reshape(-1).astype(jnp.int32)
    n_tiles = (t * TOP_K) // MOE_TM + N_EXPERTS
    tile_row = jnp.arange(n_tiles, dtype=jnp.int32)[:, None] * MOE_TM
    tile_eid = jnp.minimum(jnp.sum((gend[None, :] <= tile_row).astype(jnp.int32), axis=1), N_EXPERTS - 1)
    nused = (gend[-1:] // MOE_TM).astype(jnp.int32)
    xs = _dispatch(xf, dest_flat, gend.astype(jnp.int32), n_tiles * MOE_TM)
    ys = _experts(xs, tile_eid, nused, wgu, bgu, wd, bd, layer)
    return _combine_ln(xf, ys, dest_flat, route, g, b)


def _compress(kv, pe, w1, b1, w2, b2, t, rope_tab):
    g = A_KV_GROUPS
    nch = t // CMP_STRIDE
    n_pad = nch
    chunks = kv.reshape(nch, CMP_STRIDE, g, HEAD_DIM)
    blocks = jnp.concatenate([chunks[:-1], chunks[1:]], axis=1)
    flat = blocks.transpose(2, 0, 1, 3).reshape(g, nch - 1, CMP_LEN * HEAD_DIM)
    flat = jnp.pad(flat, ((0, 0), (0, 1), (0, 0))).reshape(g * n_pad, CMP_LEN * HEAD_DIM)
    w1b = w1.astype(BF16)
    pe_rows = jnp.zeros((8, CMP_LEN * HEAD_DIM), F32).at[0].set(pe.reshape(-1)).astype(BF16)
    pe_term = _mm(pe_rows, w1b, out_dtype=F32, name="cmp_pe")[0]
    hid = _mm(flat, w1b, bias=pe_term + b1, act="gelu", name="cmp_mlp1")
    if rope_tab is None:
        return _mm(hid, w2.astype(BF16), bias=b2, tn=LANE, name="cmp_mlp2")
    return _mm(hid, w2.astype(BF16), bias=b2, tn=LANE, rope_tabs=rope_tab[None],
               rope_modes=jnp.ones((1,), jnp.int32), name="cmp_mlp2_rope")


def _even_w_kernel(a_ref, b_ref, o_ref, *, first_shifted, gate_tile):
    j = pl.program_id(1)
    a = a_ref[...]
    tn = a.shape[1]

    @pl.when(j < first_shifted)
    def _():
        o_ref[...] = a.astype(o_ref.dtype)

    @pl.when((j >= first_shifted) & (j < gate_tile))
    def _():
        b = b_ref[...]
        o_ref[...] = jnp.concatenate([a[:, A_GATE:], b[:, :A_GATE]], axis=1).astype(o_ref.dtype)

    @pl.when(j == gate_tile)
    def _():
        lane = lax.broadcasted_iota(jnp.int32, a.shape, 1)
        o_ref[...] = jnp.where(lane < A_GATE, a, 0.0).astype(o_ref.dtype)


def _even_w_layout(w_in, e, tr=512):
    d = w_in.shape[1]
    tr = min(tr, d)
    first_shifted = (A_Q + 6 * A_KV) // MM_TN
    gate_tile = EVEN_COLS // MM_TN - 1
    last_in = (w_in.shape[2] - 1) // MM_TN
    kern = functools.partial(_even_w_kernel, first_shifted=first_shifted, gate_tile=gate_tile)
    return pl.pallas_call(
        kern, grid=(d // tr, EVEN_COLS // MM_TN),
        in_specs=[pl.BlockSpec((None, tr, MM_TN),
                               lambda i, j: (e, i, jnp.where(j == gate_tile, first_shifted, j))),
                  pl.BlockSpec((None, tr, MM_TN), lambda i, j: (e, i, jnp.minimum(j + 1, last_in)))],
        out_specs=pl.BlockSpec((tr, MM_TN), lambda i, j: (i, j)),
        out_shape=jax.ShapeDtypeStruct((d, EVEN_COLS), BF16),
        compiler_params=_cparams(("parallel", "arbitrary")), name="even_w_layout",
    )(w_in, w_in)


def _even_mixer(xb, w_in, w_out, e, cmpk, cmpv, tabs):
    t, d = xb.shape
    w = _even_w_layout(w_in, e)
    tile_modes = np.zeros((EVEN_COLS // MM_TN,), np.int32)
    per = MM_TN // LANE
    for cb, nblk, mode in ((CB_QA, A_Q // LANE, 3), (CB_KS, A_KV // LANE, 1), (CB_KW, A_KV // LANE, 1),
                           (CB_QB, B_W // LANE, 2), (CB_KB, B_W // LANE, 1)):
        tile_modes[cb // per:(cb + nblk) // per] = mode
    h = _mm(xb, w, rope_tabs=tabs["qk"], rope_modes=jnp.asarray(tile_modes), name="even_in_proj")

    n_pad = t // CMP_STRIDE
    kc = _compress(h[:, CB_KC * LANE:CB_VC * LANE], *cmpk, t, jnp.tile(tabs["cmp"], (A_KV_GROUPS, 1)))
    vc = _compress(h[:, CB_VC * LANE:CB_KS * LANE], *cmpv, t, None)
    o_c, sel = _cmp_attn(h, kc, vc, t)
    o_s = _sel_attn(h, sel, t)

    tqw = min(WIN_LEN, t)
    (o_w,) = _swa(h, h, h, n_r=A_KV_GROUPS, n_tiles=t // tqw, tq=tqw, nh=A_HPG, shared_kv=True,
                  max_dist=WIN_LEN - 1, q_map=lambda r: r, k_map=lambda r: CB_KW + r,
                  v_map=lambda r: CB_VW + r, out_cols=A_Q, o_map=lambda r: r, with_lse=False,
                  log2_scores=True,
                  name="nsa_window")

    ob_parts, lse_parts = [], []
    for gi, (window, dil) in enumerate(B_DILATIONS):
        if dil == 1:
            tqd = min(max(window, LANE), t)
            o, lse = _swa(h, h, h, n_r=1, n_tiles=t // tqd, tq=tqd, nh=B_HPG, shared_kv=False,
                          max_dist=window, q_map=lambda r, b=CB_QB // B_HPG + gi: b,
                          k_map=lambda r, b=CB_KB // B_HPG + gi: b,
                          v_map=lambda r, b=CB_VB // B_HPG + gi: b,
                          out_cols=B_HPG * LANE, o_map=lambda r: r, with_lse=True, log2_scores=False,
                          name="dilated_1")
        else:
            o, lse = _band_attn(h, t, q_cb=CB_QB + gi * B_HPG, k_cb=CB_KB + gi * B_HPG,
                                v_cb=CB_VB + gi * B_HPG, nh=B_HPG, window=window, dil=dil)
        ob_parts.append(o)
        lse_parts.append(lse)
    o_b = jnp.concatenate(ob_parts, axis=1)
    lse_b = jnp.concatenate(lse_parts, axis=1)
    mix_in = _mixprep(o_c, o_s, o_w, h, o_b, lse_b, t)
    return _mm(mix_in, w_out, out_dtype=F32, b_layer=e, name="even_out_proj")


def _odd_mixer(xb, w_in, w_out, o_idx, lq1, lk1, lq2, lk2, sub_g, layer, tabs):
    t, d = xb.shape
    per = MM_TN // LANE
    tile_modes = np.zeros(((2 * C_QK + C_V) // MM_TN,), np.int32)
    tile_modes[:C_QK // MM_TN] = 3
    tile_modes[C_QK // MM_TN:2 * C_QK // MM_TN] = 1
    h = _mm(xb, w_in, rope_tabs=tabs["qk"], rope_modes=jnp.asarray(tile_modes), b_layer=o_idx,
            name="odd_in_proj")
    o = _diff_attn(h, lq1, lk1, lq2, lk2, sub_g, layer, t)
    return _mm(o, w_out, out_dtype=F32, b_layer=o_idx, name="odd_out_proj")


def kernel(x, even_w_in, even_w_out, cmpk_pe, cmpk_w1, cmpk_b1, cmpk_w2, cmpk_b2, cmpv_pe, cmpv_w1, cmpv_b1, cmpv_w2, cmpv_b2, odd_w_in, odd_w_out, lam_q1, lam_k1, lam_q2, lam_k2, subln_g, ln_mix_g, ln_mix_b, ln_ffn_g, ln_ffn_b, router_w, router_b, exp_w_gu, exp_b_gu, exp_w_down, exp_b_down):
    bsz, t, d = x.shape
    assert bsz == 1
    xf = x.reshape(t, d)
    xb = xf.astype(BF16)
    pos = jnp.arange(t, dtype=jnp.int32)
    cmp_end = jnp.arange(t // CMP_STRIDE, dtype=jnp.int32) * CMP_STRIDE + (CMP_LEN - 1)
    tabs = {
        "qk": jnp.stack([_rope_tables(pos, 1.0), _rope_tables(pos, HEAD_DIM ** -0.5),
                         _rope_tables(pos, HEAD_DIM ** -0.5 * math.log2(math.e))]),
        "cmp": _rope_tables(cmp_end, 1.0),
    }
    even_w_out_b = even_w_out.astype(BF16)
    odd_w_in_b = odd_w_in.astype(BF16)
    odd_w_out_b = odd_w_out.astype(BF16)
    wgu_b = exp_w_gu.astype(BF16)
    wd_b = exp_w_down.astype(BF16)
    bgu = exp_b_gu[:, :, None, :].astype(F32)
    bd = exp_b_down[:, :, None, :].astype(F32)
    for layer in range(DEPTH):
        if layer % 2 == 0:
            e = layer // 2
            mix = _even_mixer(xb, even_w_in, even_w_out_b, e,
                              (cmpk_pe[e], cmpk_w1[e], cmpk_b1[e], cmpk_w2[e], cmpk_b2[e]),
                              (cmpv_pe[e], cmpv_w1[e], cmpv_b1[e], cmpv_w2[e], cmpv_b2[e]), tabs)
        else:
            o = layer // 2
            mix = _odd_mixer(xb, odd_w_in_b, odd_w_out_b, o, lam_q1[o], lam_k1[o], lam_q2[o],
                             lam_k2[o], subln_g[o], layer, tabs)
        xf, xb = _ln_res(xf, mix, ln_mix_g[layer], ln_mix_b[layer])
        xf, xb = _moe_ln(xf, router_w[layer], router_b[layer], wgu_b, bgu, wd_b, bd,
                         ln_ffn_g[layer], ln_ffn_b[layer], layer)
    return xf.reshape(bsz, t, d)
```

```python
import functools
import math

import jax
import jax.numpy as jnp
import numpy as np
from jax import lax
from jax.experimental import pallas as pl
from jax.experimental.pallas import tpu as pltpu

F32 = jnp.float32
BF16 = jnp.bfloat16

DEPTH = 4
HEAD_DIM = 128
ROPE_THETA = 500000.0
ROT_DIM = HEAD_DIM // 4
ROT_HALF = ROT_DIM // 2
NEG_INF = -1e30
SEL_FORCE = 1e9

A_HEADS = 20
A_KV_GROUPS = 4
A_HPG = A_HEADS // A_KV_GROUPS
CMP_LEN = 32
CMP_STRIDE = 16
CMP_HIDDEN = 4 * HEAD_DIM
SLC_LEN = 64
SLC_TOPK = 16
WIN_LEN = 512

B_DILATIONS = ((128, 1), (512, 4), (2048, 16))
B_HPG = 4
B_HEADS = B_HPG * len(B_DILATIONS)

C_HEADS = 16
C_VDIM = 2 * HEAD_DIM

N_EXPERTS = 32
TOP_K = 4
D_EXPERT = 384
SWIGLU_LIMIT = 7.0
SWIGLU_ALPHA = 1.702

DEEPNORM_ALPHA = (2.0 * DEPTH) ** 0.25

A_Q = A_HEADS * HEAD_DIM
A_KV = A_KV_GROUPS * HEAD_DIM
A_GATE = A_HEADS * 3
B_W = B_HEADS * HEAD_DIM
C_QK = 2 * C_HEADS * HEAD_DIM
C_V = C_HEADS * C_VDIM

LANE = 128
VMEM_LIMIT = 56 * 1024 * 1024

GATE_PAD = 512
EVEN_COLS = A_Q + 6 * A_KV + 3 * B_W + GATE_PAD
CB_QA = 0
CB_KC = A_Q // LANE
CB_VC = CB_KC + A_KV // LANE
CB_KS = CB_VC + A_KV // LANE
CB_VS = CB_KS + A_KV // LANE
CB_KW = CB_VS + A_KV // LANE
CB_VW = CB_KW + A_KV // LANE
CB_QB = CB_VW + A_KV // LANE
CB_KB = CB_QB + B_W // LANE
CB_VB = CB_KB + B_W // LANE
CB_GATE = CB_VB + B_W // LANE

MM_TN = 512


def _cparams(sem):
    return pltpu.CompilerParams(dimension_semantics=sem, vmem_limit_bytes=VMEM_LIMIT)


def _split_top16(a):
    bits = lax.bitcast_convert_type(a, jnp.uint32) & jnp.uint32(0xFFFF0000)
    hi = lax.bitcast_convert_type(bits, F32)
    return hi, a - hi


def _nt_dot(a, b):
    return lax.dot_general(a, b, (((1,), (1,)), ((), ())), preferred_element_type=F32)


def _rope_lanes(x, tab):
    c = tab[:, 0:LANE]
    sa = tab[:, LANE:2 * LANE]
    sb = tab[:, 2 * LANE:3 * LANE]
    return (x * c + pltpu.roll(x, LANE - ROT_HALF, axis=1) * sa
            + pltpu.roll(x, ROT_HALF, axis=1) * sb)


def _mm_kernel(*refs, has_bias, act, has_rope, n_sub):
    if has_rope:
        modes_ref, a_ref, b_ref = refs[0], refs[1], refs[2]
        rest = refs[3:]
    else:
        a_ref, b_ref = refs[0], refs[1]
        rest = refs[2:]
    idx = 0
    if has_bias:
        bias_ref = rest[idx]
        idx += 1
    if has_rope:
        tab_ref = rest[idx]
        idx += 1
    o_ref = rest[idx]

    acc = jnp.dot(a_ref[...], b_ref[...], preferred_element_type=F32)
    if has_bias:
        acc = acc + bias_ref[...]
    if act == "gelu":
        acc = jax.nn.gelu(acc, approximate=True)
    if not has_rope:
        o_ref[...] = acc.astype(o_ref.dtype)
        return
    mode = modes_ref[pl.program_id(1)]

    @pl.when(mode == 0)
    def _():
        o_ref[...] = acc.astype(o_ref.dtype)

    @pl.when(mode != 0)
    def _():
        tab = tab_ref[...]
        for s in range(n_sub):
            sl = slice(s * LANE, (s + 1) * LANE)
            o_ref[:, sl] = _rope_lanes(acc[:, sl], tab).astype(o_ref.dtype)


def _mm(a, b, *, bias=None, act=None, rope_tabs=None, rope_modes=None,
        out_dtype=None, tm=1024, tn=MM_TN, b_layer=None, name="mm"):
    out_dtype = BF16 if out_dtype is None else out_dtype
    m, k = a.shape
    k2, n = b.shape[-2:]
    assert k == k2 and (b.ndim == 3) == (b_layer is not None)
    tm = min(tm, m)
    tn = min(tn, n)
    assert m % tm == 0 and n % tn == 0
    has_bias = bias is not None
    has_rope = rope_tabs is not None
    grid = (m // tm, n // tn)
    kern = functools.partial(_mm_kernel, has_bias=has_bias, act=act, has_rope=has_rope,
                             n_sub=tn // LANE)
    if b_layer is None:
        b_spec = pl.BlockSpec((k, tn), lambda i, j, *_: (0, j))
    else:
        b_spec = pl.BlockSpec((None, k, tn), lambda i, j, *_: (b_layer, 0, j))
    if has_rope:
        in_specs = [pl.BlockSpec((tm, k), lambda i, j, md: (i, 0)), b_spec]
        args = [a, b]
        if has_bias:
            in_specs.append(pl.BlockSpec((1, tn), lambda i, j, md: (0, j)))
            args.append(bias.reshape(1, n).astype(F32))
        in_specs.append(pl.BlockSpec((None, tm, 3 * LANE),
                                     lambda i, j, md: (jnp.maximum(md[j] - 1, 0), i, 0)))
        args.append(rope_tabs)
        gs = pltpu.PrefetchScalarGridSpec(
            num_scalar_prefetch=1, grid=grid, in_specs=in_specs,
            out_specs=pl.BlockSpec((tm, tn), lambda i, j, md: (i, j)))
        return pl.pallas_call(kern, grid_spec=gs,
                              out_shape=jax.ShapeDtypeStruct((m, n), out_dtype),
                              compiler_params=_cparams(("parallel", "arbitrary")),
                              name=name)(rope_modes, *args)
    in_specs = [pl.BlockSpec((tm, k), lambda i, j: (i, 0)), b_spec]
    args = [a, b]
    if has_bias:
        in_specs.append(pl.BlockSpec((1, tn), lambda i, j: (0, j)))
        args.append(bias.reshape(1, n).astype(F32))
    return pl.pallas_call(kern, grid=grid, in_specs=in_specs,
                          out_specs=pl.BlockSpec((tm, tn), lambda i, j: (i, j)),
                          out_shape=jax.ShapeDtypeStruct((m, n), out_dtype),
                          compiler_params=_cparams(("parallel", "arbitrary")),
                          name=name)(*args)


def _ln_kernel(x_ref, y_ref, g_ref, b_ref, of_ref, ob_ref):
    z = DEEPNORM_ALPHA * x_ref[...] + y_ref[...].astype(F32)
    mu = jnp.mean(z, axis=-1, keepdims=True)
    zc = z - mu
    var = jnp.mean(zc * zc, axis=-1, keepdims=True)
    out = zc * lax.rsqrt(var + 1e-5) * g_ref[...] + b_ref[...]
    of_ref[...] = out
    ob_ref[...] = out.astype(BF16)


def _ln_res(x, y, g, b, tm=256):
    t, d = x.shape
    tm = min(tm, t)
    row = pl.BlockSpec((tm, d), lambda i: (i, 0))
    vec = pl.BlockSpec((1, d), lambda i: (0, 0))
    return pl.pallas_call(
        _ln_kernel, grid=(t // tm,), in_specs=[row, row, vec, vec], out_specs=[row, row],
        out_shape=[jax.ShapeDtypeStruct((t, d), F32), jax.ShapeDtypeStruct((t, d), BF16)],
        compiler_params=_cparams(("parallel",)), name="ln_res",
    )(x, y, g.reshape(1, d), b.reshape(1, d))


def _rope_tables(pos, scale):
    inv = ROPE_THETA ** (-jnp.arange(ROT_HALF, dtype=F32) / ROT_HALF)
    ang = pos.astype(F32)[:, None] * inv[None, :]
    cos = jnp.cos(ang)
    sin = jnp.sin(ang)
    n = pos.shape[0]
    ones = jnp.ones((n, HEAD_DIM - ROT_DIM), F32)
    zer = jnp.zeros((n, HEAD_DIM - ROT_HALF), F32)
    c = jnp.concatenate([cos, cos, ones], axis=1)
    sa = jnp.concatenate([-sin, zer], axis=1)
    sb = jnp.concatenate([jnp.zeros((n, ROT_HALF), F32), sin,
                          jnp.zeros((n, HEAD_DIM - ROT_DIM), F32)], axis=1)
    return jnp.concatenate([c, sa, sb], axis=1) * scale


def _swa_kernel(q_ref, kp_ref, kc_ref, vp_ref, vc_ref, *out_refs, nh, shared_kv, max_dist,
                tq, with_lse, log2_scores):
    assert not (with_lse and log2_scores)
    o_ref = out_refs[0]
    i = pl.program_id(1)
    row = lax.broadcasted_iota(jnp.int32, (tq, 2 * tq), 0)
    col = lax.broadcasted_iota(jnp.int32, (tq, 2 * tq), 1)
    dist = row + tq - col
    mask = (dist >= 0) & (dist <= max_dist) & (col + (i - 1) * tq >= 0)
    for h in range(nh):
        kv = 0 if shared_kv else h
        hs = slice(h * LANE, (h + 1) * LANE)
        ks = slice(kv * LANE, (kv + 1) * LANE)
        q = q_ref[:, hs]
        k = jnp.concatenate([kp_ref[:, ks], kc_ref[:, ks]], axis=0)
        v = jnp.concatenate([vp_ref[:, ks], vc_ref[:, ks]], axis=0)
        s = jnp.where(mask, _nt_dot(q, k), NEG_INF)
        m = jnp.max(s, axis=-1, keepdims=True)
        e = jnp.exp2(s - m) if log2_scores else jnp.exp(s - m)
        den = jnp.sum(e, axis=-1, keepdims=True)
        o = jnp.dot(e.astype(BF16), v, preferred_element_type=F32) / den
        o_ref[:, hs] = o.astype(o_ref.dtype)
        if with_lse:
            out_refs[1][:, hs] = jnp.broadcast_to(m + jnp.log(den), (tq, LANE))


def _swa(qsrc, ksrc, vsrc, *, n_r, n_tiles, tq, nh, shared_kv, max_dist, q_map, k_map, v_map,
         out_cols, o_map, with_lse, log2_scores, name):
    kvw = LANE if shared_kv else nh * LANE
    qw = nh * LANE
    prev = lambda f: (lambda r, i: (jnp.maximum(i - 1, 0), f(r)))
    cur = lambda f: (lambda r, i: (i, f(r)))
    in_specs = [pl.BlockSpec((tq, qw), cur(q_map)),
                pl.BlockSpec((tq, kvw), prev(k_map)), pl.BlockSpec((tq, kvw), cur(k_map)),
                pl.BlockSpec((tq, kvw), prev(v_map)), pl.BlockSpec((tq, kvw), cur(v_map))]
    rows = n_tiles * tq
    out_shape = [jax.ShapeDtypeStruct((rows, out_cols), BF16)]
    out_specs = [pl.BlockSpec((tq, qw), cur(o_map))]
    if with_lse:
        out_shape.append(jax.ShapeDtypeStruct((rows, out_cols), F32))
        out_specs.append(pl.BlockSpec((tq, qw), cur(o_map)))
    kern = functools.partial(_swa_kernel, nh=nh, shared_kv=shared_kv, max_dist=max_dist, tq=tq,
                             with_lse=with_lse, log2_scores=log2_scores)
    return pl.pallas_call(kern, grid=(n_r, n_tiles), in_specs=in_specs, out_specs=out_specs,
                          out_shape=out_shape,
                          compiler_params=_cparams(("parallel", "arbitrary")),
                          name=name)(qsrc, ksrc, ksrc, vsrc, vsrc)


def _band_kernel(q_ref, k_ref, v_ref, o_ref, lse_ref, *, tq, band, window, dil):
    i = pl.program_id(1)
    start = pl.multiple_of(jnp.maximum((i + 1) * tq - band, 0), tq)
    k = k_ref[pl.ds(start, band), :]
    v = v_ref[pl.ds(start, band), :]
    dist = (lax.broadcasted_iota(jnp.int32, (tq, band), 0)
            - lax.broadcasted_iota(jnp.int32, (tq, band), 1)) + (i * tq - start)
    valid = (dist >= 0) & (dist <= window) & ((dist & (dil - 1)) == 0)
    s = jnp.where(valid, _nt_dot(q_ref[...], k), NEG_INF)
    m = jnp.max(s, axis=-1, keepdims=True)
    e = jnp.exp(s - m)
    l = jnp.sum(e, axis=-1, keepdims=True)
    o_ref[...] = (jnp.dot(e.astype(BF16), v, preferred_element_type=F32) / l).astype(o_ref.dtype)
    lse_ref[...] = jnp.broadcast_to(m + jnp.log(l), (tq, LANE))


def _band_attn(h, t, *, q_cb, k_cb, v_cb, nh, window, dil, tq=256):
    tq = min(tq, t)
    assert dil & (dil - 1) == 0 and window % tq == 0
    band = min(window + tq, t)
    kern = functools.partial(_band_kernel, tq=tq, band=band, window=window, dil=dil)
    blk = pl.BlockSpec((tq, LANE), lambda hd, i: (i, hd))
    return pl.pallas_call(
        kern, grid=(nh, t // tq),
        in_specs=[pl.BlockSpec((tq, LANE), lambda hd, i: (i, q_cb + hd)),
                  pl.BlockSpec((t, LANE), lambda hd, i: (0, k_cb + hd)),
                  pl.BlockSpec((t, LANE), lambda hd, i: (0, v_cb + hd))],
        out_specs=[blk, blk],
        out_shape=[jax.ShapeDtypeStruct((t, nh * LANE), BF16), jax.ShapeDtypeStruct((t, nh * LANE), F32)],
        compiler_params=_cparams(("parallel", "arbitrary")), name="dilated_band_%d" % dil,
    )(h, h, h)


def _cmp_kernel(q_ref, kc_ref, vc_ref, ov_ref, oc_ref, sel_ref, *, tq, n_cmp_pad, n_slc):
    i = pl.program_id(1)
    tqv = i * tq + lax.broadcasted_iota(jnp.int32, (tq, 1), 0)
    cmp_end = CMP_STRIDE * lax.broadcasted_iota(jnp.int32, (1, n_cmp_pad), 1) + (CMP_LEN - 1)
    maskc = cmp_end <= tqv
    kc = kc_ref[...]
    vc = vc_ref[...]
    psum = jnp.zeros((tq, n_cmp_pad), F32)
    for h in range(A_HPG):
        hs = slice(h * LANE, (h + 1) * LANE)
        s = jnp.where(maskc, _nt_dot(q_ref[:, hs], kc), NEG_INF)
        m = jnp.max(s, axis=-1, keepdims=True)
        e = jnp.where(maskc, jnp.exp2(s - m), 0.0)
        den = jnp.sum(e, axis=-1, keepdims=True)
        p = e / jnp.maximum(den, 1e-30)
        oc_ref[:, hs] = jnp.dot(p.astype(BF16), vc, preferred_element_type=F32).astype(oc_ref.dtype)
        psum = psum + p
    ov = ov_ref[...]
    p1, r1 = _split_top16(psum)
    p2, r2 = _split_top16(r1)
    imp = (jnp.dot(p1.astype(BF16), ov, preferred_element_type=F32)
           + jnp.dot(p2.astype(BF16), ov, preferred_element_type=F32)
           + jnp.dot(r2.astype(BF16), ov, preferred_element_type=F32))
    j = lax.broadcasted_iota(jnp.int32, (1, n_slc), 1)
    jf = j.astype(F32)
    tb = tqv // SLC_LEN
    forced = (j == 0) | (j == tb) | (j == tb - 1)
    val = jnp.where(forced, SEL_FORCE, jnp.where(j <= tb, imp, -SEL_FORCE))
    sel = jnp.zeros((tq, n_slc), F32)
    for _ in range(min(SLC_TOPK, n_slc)):
        m = jnp.max(val, axis=-1, keepdims=True)
        first = jnp.min(jnp.where(val == m, jf, float(n_slc)), axis=-1, keepdims=True)
        hit = jf == first
        sel = jnp.where(hit & (m > -0.5 * SEL_FORCE), 1.0, sel)
        val = jnp.where(hit, -3e38, val)
    sel_ref[...] = sel.astype(sel_ref.dtype)


def _cmp_attn(h, kc, vc, t, tq=512):
    n_cmp_pad = kc.shape[0] // A_KV_GROUPS
    n_slc = t // SLC_LEN
    ci = np.arange(n_cmp_pad)[:, None] * CMP_STRIDE
    sj = np.arange(n_slc)[None, :] * SLC_LEN
    overlap = ((ci < sj + SLC_LEN) & (ci + CMP_LEN > sj) & (np.arange(n_cmp_pad)[:, None] < t // CMP_STRIDE - 1))
    overlap = jnp.asarray(overlap.astype(np.float32), BF16)
    kern = functools.partial(_cmp_kernel, tq=tq, n_cmp_pad=n_cmp_pad, n_slc=n_slc)
    qw = A_HPG * LANE
    return pl.pallas_call(
        kern, grid=(A_KV_GROUPS, t // tq),
        in_specs=[pl.BlockSpec((tq, qw), lambda g, i: (i, g)),
                  pl.BlockSpec((n_cmp_pad, LANE), lambda g, i: (g, 0)),
                  pl.BlockSpec((n_cmp_pad, LANE), lambda g, i: (g, 0)),
                  pl.BlockSpec((n_cmp_pad, n_slc), lambda g, i: (0, 0))],
        out_specs=[pl.BlockSpec((tq, qw), lambda g, i: (i, g)),
                   pl.BlockSpec((tq, n_slc), lambda g, i: (i, g))],
        out_shape=[jax.ShapeDtypeStruct((t, A_Q), BF16),
                   jax.ShapeDtypeStruct((t, A_KV_GROUPS * n_slc), BF16)],
        compiler_params=_cparams(("parallel", "arbitrary")), name="cmp_attn_topk",
    )(h, kc, vc, overlap)


ROW_CHUNK = 32


def _flash_pipeline(n, score_stage, softmax_stage, value_stage):
    @pl.when(n > 0)
    def _():
        score_stage(0, 0)

    def step(t, slot):
        softmax_stage(slot)
        value_stage(1 - slot, jnp.maximum(t - 1, 0))
        score_stage(1 - slot, jnp.minimum(t + 1, n - 1))

    def pair(u, c):
        step(2 * u, 0)
        step(2 * u + 1, 1)
        return c

    lax.fori_loop(0, n // 2, pair, 0)
    last = jnp.maximum(n - 1, 0)

    @pl.when(n % 2 == 1)
    def _():
        step(n - 1, 0)
        value_stage(0, last)

    @pl.when(n % 2 == 0)
    def _():
        value_stage(1, last)


def _sel_kernel(q_ref, k_ref, v_ref, sel_ref, o_ref, q5_sc, s_sc, bias_sc, p_sc, acc_sc, m_sc, l_sc,
                alpha_sc, *, tq, tk, n_slc):
    i = pl.program_id(1)
    nh = A_HPG
    for h in range(nh):
        q5_sc[h * tq:(h + 1) * tq, :] = q_ref[:, h * LANE:(h + 1) * LANE]
    m_sc[...] = jnp.full(m_sc.shape, NEG_INF, F32)
    l_sc[...] = jnp.zeros_like(l_sc)
    acc_sc[...] = jnp.zeros_like(acc_sc)
    p_sc[1] = jnp.zeros(p_sc.shape[1:], p_sc.dtype)
    alpha_sc[1] = jnp.ones(alpha_sc.shape[1:], F32)
    selb = sel_ref[...]
    tqv = i * tq + lax.broadcasted_iota(jnp.int32, (tq, 1), 0)
    blk_per_tile = tk // SLC_LEN
    blk_gap = (lax.broadcasted_iota(jnp.int32, (n_slc, tk), 0)
               - lax.broadcasted_iota(jnp.int32, (n_slc, tk), 1) // SLC_LEN)
    kcol = lax.broadcasted_iota(jnp.int32, (1, tk), 1)
    n_kv = (i * tq + tq + tk - 1) // tk
    nrep = tk // LANE

    def score_stage(slot, t):
        k = k_ref[pl.ds(pl.multiple_of(t * tk, tk), tk), :]
        s_sc[slot] = _nt_dot(q5_sc[...], k)
        expand = jnp.where(blk_gap == t * blk_per_tile, 1.0, 0.0).astype(BF16)
        picked = jnp.dot(selb, expand, preferred_element_type=F32)
        bias_sc[slot] = jnp.where((picked > 0.5) & (kcol + t * tk <= tqv), 0.0, NEG_INF)

    def softmax_stage(slot):
        for c in range(nh * tq // ROW_CHUNK):
            rows = slice(c * ROW_CHUNK, (c + 1) * ROW_CHUNK)
            r0 = (c * ROW_CHUNK) % tq
            s = s_sc[slot, rows, :] + bias_sc[slot, r0:r0 + ROW_CHUNK, :]
            m_old = m_sc[rows, :]
            m_new = jnp.maximum(m_old, jnp.max(s, axis=-1, keepdims=True))
            alpha = jnp.exp2(m_old - m_new)
            p = jnp.exp2(s - jnp.concatenate([m_new] * nrep, axis=1))
            l_sc[rows, :] = alpha * l_sc[rows, :] + jnp.sum(p, axis=-1, keepdims=True)
            m_sc[rows, :] = m_new
            alpha_sc[slot, rows, :] = alpha
            p_sc[slot, rows, :] = p.astype(p_sc.dtype)

    def value_stage(slot, t):
        v = v_ref[pl.ds(pl.multiple_of(t * tk, tk), tk), :]
        acc_sc[...] = acc_sc[...] * alpha_sc[slot] + jnp.dot(p_sc[slot], v, preferred_element_type=F32)

    _flash_pipeline(n_kv, score_stage, softmax_stage, value_stage)
    o = acc_sc[...] / l_sc[...]
    for h in range(nh):
        o_ref[:, h * LANE:(h + 1) * LANE] = o[h * tq:(h + 1) * tq].astype(o_ref.dtype)


def _sel_attn(h, sel, t, tq=128, tk=512):
    n_slc = t // SLC_LEN
    tk = min(tk, t)
    qw = A_HPG * LANE
    rows = A_HPG * tq
    kern = functools.partial(_sel_kernel, tq=tq, tk=tk, n_slc=n_slc)
    return pl.pallas_call(
        kern, grid=(A_KV_GROUPS, t // tq),
        in_specs=[pl.BlockSpec((tq, qw), lambda g, i: (i, g)),
                  pl.BlockSpec((t, LANE), lambda g, i: (0, CB_KS + g)),
                  pl.BlockSpec((t, LANE), lambda g, i: (0, CB_VS + g)),
                  pl.BlockSpec((tq, n_slc), lambda g, i: (i, g))],
        out_specs=pl.BlockSpec((tq, qw), lambda g, i: (i, g)),
        out_shape=jax.ShapeDtypeStruct((t, A_Q), BF16),
        scratch_shapes=[pltpu.VMEM((rows, LANE), BF16), pltpu.VMEM((2, rows, tk), F32),
                        pltpu.VMEM((2, tq, tk), F32), pltpu.VMEM((2, rows, tk), BF16),
                        pltpu.VMEM((rows, LANE), F32), pltpu.VMEM((rows, LANE), F32),
                        pltpu.VMEM((rows, LANE), F32), pltpu.VMEM((2, rows, LANE), F32)],
        compiler_params=_cparams(("parallel", "arbitrary")), name="sel_attn",
    )(h, h, h, sel)


def _mixprep_kernel(oc_ref, os_ref, ow_ref, gate_ref, ob_ref, lse_ref, out_ref):
    gate = jax.nn.sigmoid(gate_ref[...].astype(F32))
    for h in range(A_HEADS):
        hs = slice(h * LANE, (h + 1) * LANE)
        o = (gate[:, 3 * h:3 * h + 1] * oc_ref[:, hs].astype(F32)
             + gate[:, 3 * h + 1:3 * h + 2] * os_ref[:, hs].astype(F32)
             + gate[:, 3 * h + 2:3 * h + 3] * ow_ref[:, hs].astype(F32))
        out_ref[:, hs] = o.astype(out_ref.dtype)
    ng = len(B_DILATIONS)
    for hi in range(B_HPG):
        lses = [lse_ref[:, (g * B_HPG + hi) * LANE:(g * B_HPG + hi + 1) * LANE] for g in range(ng)]
        m = functools.reduce(jnp.maximum, lses)
        es = [jnp.exp(x - m) for x in lses]
        den = functools.reduce(lambda a, b: a + b, es)
        for g in range(ng):
            src = slice((g * B_HPG + hi) * LANE, (g * B_HPG + hi + 1) * LANE)
            dst = slice(A_Q + (g * B_HPG + hi) * LANE, A_Q + (g * B_HPG + hi + 1) * LANE)
            out_ref[:, dst] = (ob_ref[:, src].astype(F32) * (es[g] / den)).astype(out_ref.dtype)


def _mixprep(o_c, o_s, o_w, h, o_b, lse_b, t, tq=256):
    tq = min(tq, t)
    a = pl.BlockSpec((tq, A_Q), lambda i: (i, 0))
    b = pl.BlockSpec((tq, B_W), lambda i: (i, 0))
    return pl.pallas_call(
        _mixprep_kernel, grid=(t // tq,),
        in_specs=[a, a, a, pl.BlockSpec((tq, LANE), lambda i: (i, CB_GATE)), b, b],
        out_specs=pl.BlockSpec((tq, A_Q + B_W), lambda i: (i, 0)),
        out_shape=jax.ShapeDtypeStruct((t, A_Q + B_W), BF16),
        compiler_params=_cparams(("parallel",)), name="even_mixprep",
    )(o_c, o_s, o_w, h, o_b, lse_b)


def _diff_kernel(q_ref, k_ref, v_ref, lq1_ref, lk1_ref, lq2_ref, lk2_ref, g_ref, o_ref,
                 s_sc, p_sc, acc_sc, m_sc, l_sc, alpha_sc, *, tq, tk, lam_init):
    i = pl.program_id(1)
    q1 = q_ref[:, 0:LANE]
    q2 = q_ref[:, LANE:2 * LANE]
    lam = (jnp.exp(jnp.sum(lq1_ref[...] * lk1_ref[...], axis=-1, keepdims=True))
           - jnp.exp(jnp.sum(lq2_ref[...] * lk2_ref[...], axis=-1, keepdims=True)) + lam_init)

    m_sc[...] = jnp.full(m_sc.shape, NEG_INF, F32)
    l_sc[...] = jnp.zeros_like(l_sc)
    acc_sc[...] = jnp.zeros_like(acc_sc)
    p_sc[1] = jnp.zeros(p_sc.shape[1:], p_sc.dtype)
    alpha_sc[1] = jnp.ones(alpha_sc.shape[1:], F32)
    nrep = tk // LANE
    rep = C_VDIM // LANE

    def score_stage(slot, start):
        k = k_ref[pl.ds(start, tk), :]
        s_sc[slot, 0] = _nt_dot(q1, k[:, 0:LANE])
        s_sc[slot, 1] = _nt_dot(q2, k[:, LANE:2 * LANE])

    def softmax_stage(slot, diag_off):
        for hd in range(2):
            for c in range(tq // ROW_CHUNK):
                rows = slice(c * ROW_CHUNK, (c + 1) * ROW_CHUNK)
                s = s_sc[slot, hd, rows, :]
                if diag_off is not None:
                    row = lax.broadcasted_iota(jnp.int32, (ROW_CHUNK, tk), 0) + c * ROW_CHUNK
                    col = lax.broadcasted_iota(jnp.int32, (ROW_CHUNK, tk), 1) + diag_off
                    s = jnp.where(col <= row, s, NEG_INF)
                m_old = m_sc[hd, rows, :]
                m_new = jnp.maximum(m_old, jnp.max(s, axis=-1, keepdims=True))
                alpha = jnp.exp2(m_old - m_new)
                p = jnp.exp2(s - jnp.concatenate([m_new] * nrep, axis=1))
                l_sc[hd, rows, :] = alpha * l_sc[hd, rows, :] + jnp.sum(p, axis=-1, keepdims=True)
                m_sc[hd, rows, :] = m_new
                alpha_sc[slot, hd, rows, :] = alpha
                p_sc[slot, hd, rows, :] = p.astype(p_sc.dtype)

    def value_stage(slot, start):
        v = v_ref[pl.ds(start, tk), :]
        for hd in range(2):
            pv = jnp.dot(p_sc[slot, hd], v, preferred_element_type=F32)
            alpha = alpha_sc[slot, hd]
            acc_sc[hd] = acc_sc[hd] * jnp.concatenate([alpha] * rep, axis=1) + pv

    per = tq // tk
    n_full = i * per
    _flash_pipeline(n_full, lambda slot, t: score_stage(slot, pl.multiple_of(t * tk, tk)),
                    lambda slot: softmax_stage(slot, None),
                    lambda slot, t: value_stage(slot, pl.multiple_of(t * tk, tk)))
    for u in range(per):
        start = pl.multiple_of(i * tq + u * tk, tk)
        score_stage(0, start)
        softmax_stage(0, u * tk)
        value_stage(0, start)
    l1 = jnp.concatenate([l_sc[0]] * rep, axis=1)
    l2 = jnp.concatenate([l_sc[1]] * rep, axis=1)
    o = acc_sc[0] / l1 - lam * (acc_sc[1] / l2)
    o = o * lax.rsqrt(jnp.mean(o * o, axis=-1, keepdims=True) + 1e-5) * g_ref[...] * (1.0 - lam_init)
    o_ref[...] = o.astype(o_ref.dtype)


def _diff_attn(h, lq1, lk1, lq2, lk2, sub_g, layer, t, tq=512, tk=512):
    tq = min(tq, t)
    tk = min(tk, tq)
    lam_init = 0.8 - 0.6 * math.exp(-0.3 * layer)
    kern = functools.partial(_diff_kernel, tq=tq, tk=tk, lam_init=lam_init)
    vec = pl.BlockSpec((1, LANE), lambda hh, i: (0, 0))
    qkb = C_QK // C_VDIM
    return pl.pallas_call(
        kern, grid=(C_HEADS, t // tq),
        in_specs=[pl.BlockSpec((tq, C_VDIM), lambda hh, i: (i, hh)),
                  pl.BlockSpec((t, C_VDIM), lambda hh, i: (0, qkb + hh)),
                  pl.BlockSpec((t, C_VDIM), lambda hh, i: (0, 2 * qkb + hh)),
                  vec, vec, vec, vec, pl.BlockSpec((1, C_VDIM), lambda hh, i: (0, 0))],
        out_specs=pl.BlockSpec((tq, C_VDIM), lambda hh, i: (i, hh)),
        out_shape=jax.ShapeDtypeStruct((t, C_V), BF16),
        scratch_shapes=[pltpu.VMEM((2, 2, tq, tk), F32), pltpu.VMEM((2, 2, tq, tk), BF16),
                        pltpu.VMEM((2, tq, C_VDIM), F32), pltpu.VMEM((2, tq, LANE), F32),
                        pltpu.VMEM((2, tq, LANE), F32), pltpu.VMEM((2, 2, tq, LANE), F32)],
        compiler_params=_cparams(("parallel", "arbitrary")), name="diff_attn",
    )(h, h, h, lq1.reshape(1, LANE).astype(F32), lk1.reshape(1, LANE).astype(F32),
      lq2.reshape(1, LANE).astype(F32), lk2.reshape(1, LANE).astype(F32),
      sub_g.reshape(1, C_VDIM).astype(F32))


ROUTE_EID = 0
ROUTE_RANK = 4
ROUTE_W = 8
MOE_TM = 256


def _router_kernel(x_ref, whi_ref, wlo_ref, b_ref, route_ref, cnt_ref, carry_ref):
    i = pl.program_id(0)

    @pl.when(i == 0)
    def _():
        carry_ref[...] = jnp.zeros_like(carry_ref)

    x_hi, x_lo = _split_top16(x_ref[...])
    x_hi = x_hi.astype(BF16)
    x_lo = x_lo.astype(BF16)
    logits = (jnp.dot(x_hi, whi_ref[...], preferred_element_type=F32)
              + jnp.dot(x_lo, whi_ref[...], preferred_element_type=F32)
              + jnp.dot(x_hi, wlo_ref[...], preferred_element_type=F32)
              + jnp.dot(x_lo, wlo_ref[...], preferred_element_type=F32)) + b_ref[...]
    tm, ne = logits.shape
    lane = lax.broadcasted_iota(jnp.int32, (1, ne), 1)
    jf = lane.astype(F32)
    val = logits
    tops, hits, firsts = [], [], []
    for _ in range(TOP_K):
        m = jnp.max(val, axis=-1, keepdims=True)
        first = jnp.min(jnp.where(val == m, jf, float(ne)), axis=-1, keepdims=True)
        hit = jf == first
        tops.append(m)
        hits.append(hit)
        firsts.append(first)
        val = jnp.where(hit, -3e38, val)
    es = [jnp.exp(x - tops[0]) for x in tops]
    den = functools.reduce(lambda a, b: a + b, es)
    picked = jnp.zeros((tm, ne), F32)
    for hit in hits:
        picked = jnp.where(hit, 1.0, picked)
    row = lax.broadcasted_iota(jnp.int32, (tm, tm), 0)
    col = lax.broadcasted_iota(jnp.int32, (tm, tm), 1)
    tri = jnp.where(row > col, 1.0, 0.0).astype(BF16)
    rank = jnp.dot(tri, picked.astype(BF16), preferred_element_type=F32) + carry_ref[...]
    carry_ref[...] += jnp.sum(picked, axis=0, keepdims=True)
    cnt_ref[...] = carry_ref[...]
    route = jnp.zeros((tm, ne), F32)
    for k in range(TOP_K):
        rk = jnp.sum(jnp.where(hits[k], rank, 0.0), axis=-1, keepdims=True)
        route = jnp.where(lane == ROUTE_EID + k, firsts[k], route)
        route = jnp.where(lane == ROUTE_RANK + k, rk, route)
        route = jnp.where(lane == ROUTE_W + k, es[k] / den, route)
    route_ref[...] = route


def _router(x, rw, rb, tm=256):
    t, d = x.shape
    tm = min(tm, t)
    rw_p = jnp.pad(rw.astype(F32), ((0, 0), (0, LANE - N_EXPERTS)))
    rb_p = jnp.pad(rb.astype(F32), (0, LANE - N_EXPERTS), constant_values=NEG_INF).reshape(1, LANE)
    w_hi, w_lo = _split_top16(rw_p)
    w_hi = w_hi.astype(BF16)
    w_lo = w_lo.astype(BF16)
    wblk = pl.BlockSpec((d, LANE), lambda i: (0, 0))
    return pl.pallas_call(
        _router_kernel, grid=(t // tm,),
        in_specs=[pl.BlockSpec((tm, d), lambda i: (i, 0)), wblk, wblk,
                  pl.BlockSpec((1, LANE), lambda i: (0, 0))],
        out_specs=[pl.BlockSpec((tm, LANE), lambda i: (i, 0)), pl.BlockSpec((1, LANE), lambda i: (0, 0))],
        out_shape=[jax.ShapeDtypeStruct((t, LANE), F32), jax.ShapeDtypeStruct((1, LANE), F32)],
        scratch_shapes=[pltpu.VMEM((1, LANE), F32)],
        compiler_params=_cparams(("arbitrary",)), name="moe_router",
    )(x, w_hi, w_lo, rb_p)


def _row_copy(src, src_row, dst, dst_row, sem):
    return pltpu.make_async_copy(src.at[pl.ds(src_row, 1)], dst.at[pl.ds(dst_row, 1)], sem)


def _dispatch_kernel(gend_ref, dest_ref, x_ref, xs_hbm, zeros_ref, sem, *, tm):
    i = pl.program_id(0)

    @pl.when(i == 0)
    def _():
        zeros_ref[...] = jnp.zeros_like(zeros_ref)

        def clear(e):
            start = pl.multiple_of(gend_ref[e] - MOE_TM, MOE_TM)
            return pltpu.make_async_copy(zeros_ref, xs_hbm.at[pl.ds(start, MOE_TM)], sem)
        for e in range(N_EXPERTS):
            clear(e).start()
        for e in range(N_EXPERTS):
            clear(e).wait()

        def clear_tail(j, c):
            cp = pltpu.make_async_copy(
                zeros_ref, xs_hbm.at[pl.ds(pl.multiple_of(j * MOE_TM, MOE_TM), MOE_TM)], sem)
            cp.start()
            cp.wait()
            return c

        lax.fori_loop(gend_ref[N_EXPERTS - 1] // MOE_TM, xs_hbm.shape[0] // MOE_TM, clear_tail, 0)

    def issue(r, c):
        for k in range(TOP_K):
            _row_copy(x_ref, r, xs_hbm, dest_ref[TOP_K * r + k], sem).start()
        return c

    lax.fori_loop(0, tm, issue, 0)

    def drain(r, c):
        for k in range(TOP_K):
            _row_copy(x_ref, 0, xs_hbm, 0, sem).wait()
        return c

    lax.fori_loop(0, tm, drain, 0)


def _dispatch(xf, dest_flat, gend, n_rows, tm=256):
    t, d = xf.shape
    tm = min(tm, t)
    gs = pltpu.PrefetchScalarGridSpec(
        num_scalar_prefetch=1, grid=(t // tm,),
        in_specs=[pl.BlockSpec((TOP_K * tm,), lambda i, ge: (i,), memory_space=pltpu.SMEM),
                  pl.BlockSpec((tm, d), lambda i, ge: (i, 0))],
        out_specs=pl.BlockSpec(memory_space=pl.ANY),
        scratch_shapes=[pltpu.VMEM((MOE_TM, d), F32), pltpu.SemaphoreType.DMA(())])
    return pl.pallas_call(
        functools.partial(_dispatch_kernel, tm=tm), grid_spec=gs, out_shape=jax.ShapeDtypeStruct((n_rows, d), F32),
        compiler_params=_cparams(("arbitrary",)), name="moe_dispatch",
    )(gend, dest_flat, xf)


def _expert_kernel(eid_ref, nused_ref, xs_ref, wgu_ref, bgu_ref, wd_ref, bd_ref, pick_ref, ys_ref):
    j = pl.program_id(0)

    @pl.when(j < nused_ref[0])
    def _():
        x = xs_ref[...].astype(BF16)
        hgu = jnp.dot(x, wgu_ref[...], preferred_element_type=F32) + bgu_ref[...]
        glu = jnp.minimum(hgu, SWIGLU_LIMIT)
        gated = glu * jax.nn.sigmoid(SWIGLU_ALPHA * glu)
        lin = jnp.clip(hgu, -SWIGLU_LIMIT, SWIGLU_LIMIT) + 1.0
        prod = jnp.concatenate(
            [gated[:, s:s + LANE] * pltpu.roll(lin[:, s:s + LANE], LANE - 1, axis=1)
             for s in range(0, 2 * D_EXPERT, LANE)], axis=1)
        act = jnp.dot(prod.astype(BF16), pick_ref[...], preferred_element_type=F32)
        ys_ref[...] = jnp.dot(act.astype(BF16), wd_ref[...], preferred_element_type=F32) + bd_ref[...]

    @pl.when(j >= nused_ref[0])
    def _():
        ys_ref[...] = jnp.zeros_like(ys_ref)


def _experts(xs, tile_eid, nused, wgu, bgu, wd, bd, layer):
    n_rows, d = xs.shape
    n_tiles = n_rows // MOE_TM
    pick = np.zeros((2 * D_EXPERT, D_EXPERT), np.float32)
    pick[2 * np.arange(D_EXPERT), np.arange(D_EXPERT)] = 1.0
    tile = lambda j, eid, nu: (jnp.minimum(j, nu[0] - 1), 0)
    per_expert = lambda j, eid, nu: (layer, eid[j], 0, 0)
    gs = pltpu.PrefetchScalarGridSpec(
        num_scalar_prefetch=2, grid=(n_tiles,),
        in_specs=[pl.BlockSpec((MOE_TM, d), tile),
                  pl.BlockSpec((None, None, d, 2 * D_EXPERT), per_expert),
                  pl.BlockSpec((None, None, 1, 2 * D_EXPERT), per_expert),
                  pl.BlockSpec((None, None, D_EXPERT, d), per_expert),
                  pl.BlockSpec((None, None, 1, d), per_expert),
                  pl.BlockSpec((2 * D_EXPERT, D_EXPERT), lambda j, eid, nu: (0, 0))],
        out_specs=pl.BlockSpec((MOE_TM, d), lambda j, eid, nu: (j, 0)))
    return pl.pallas_call(
        _expert_kernel, grid_spec=gs, out_shape=jax.ShapeDtypeStruct((n_rows, d), F32),
        compiler_params=_cparams(("arbitrary",)), name="moe_experts",
    )(tile_eid, nused, xs, wgu, bgu, wd, bd, jnp.asarray(pick, BF16))


def _combine_ln_kernel(dest_cur_ref, dest_nxt_ref, x_ref, route_ref, g_ref, b_ref, ys_hbm,
                       of_ref, ob_ref, ybuf, sems, *, tm):
    i = pl.program_id(0)
    slot = i % 2

    per_row = LANE // TOP_K

    def gather(dest_ref, s):
        for q in range(tm // per_row):
            def issue(rr, c, q=q):
                for k in range(TOP_K):
                    src_row = dest_ref[q, TOP_K * rr + k]
                    _row_copy(ys_hbm, src_row, ybuf.at[s, k], q * per_row + rr, sems.at[s]).start()
                return c
            lax.fori_loop(0, per_row, issue, 0)

    @pl.when(i == 0)
    def _():
        gather(dest_cur_ref, slot)

    @pl.when(i + 1 < pl.num_programs(0))
    def _():
        gather(dest_nxt_ref, 1 - slot)

    def drain(r, c):
        for k in range(TOP_K):
            _row_copy(ys_hbm, 0, ybuf.at[slot, k], 0, sems.at[slot]).wait()
        return c

    lax.fori_loop(0, tm, drain, 0)
    route = route_ref[...]
    z = DEEPNORM_ALPHA * x_ref[...]
    for k in range(TOP_K):
        z = z + route[:, ROUTE_W + k:ROUTE_W + k + 1] * ybuf[slot, k]
    mu = jnp.mean(z, axis=-1, keepdims=True)
    zc = z - mu
    var = jnp.mean(zc * zc, axis=-1, keepdims=True)
    out = zc * lax.rsqrt(var + 1e-5) * g_ref[...] + b_ref[...]
    of_ref[...] = out
    ob_ref[...] = out.astype(BF16)


def _combine_ln(xf, ys, dest_flat, route, g, b, tm=128):
    t, d = xf.shape
    tm = min(tm, t)
    n = t // tm
    row = pl.BlockSpec((tm, d), lambda i: (i, 0))
    lanes = pl.BlockSpec((tm, LANE), lambda i: (i, 0))
    vec = pl.BlockSpec((1, d), lambda i: (0, 0))
    kern = functools.partial(_combine_ln_kernel, tm=tm)
    idx_rows = TOP_K * tm // LANE
    dest3 = dest_flat.reshape(n, idx_rows, LANE)
    idx_blk = lambda f: pl.BlockSpec((None, idx_rows, LANE), f, memory_space=pltpu.SMEM)
    return pl.pallas_call(
        kern, grid=(n,),
        in_specs=[idx_blk(lambda i: (i, 0, 0)), idx_blk(lambda i: (jnp.minimum(i + 1, n - 1), 0, 0)),
                  row, lanes, vec, vec, pl.BlockSpec(memory_space=pl.ANY)],
        out_specs=[row, row],
        out_shape=[jax.ShapeDtypeStruct((t, d), F32), jax.ShapeDtypeStruct((t, d), BF16)],
        scratch_shapes=[pltpu.VMEM((2, TOP_K, tm, d), F32), pltpu.SemaphoreType.DMA((2,))],
        compiler_params=_cparams(("arbitrary",)), name="moe_combine_ln",
    )(dest3, dest3, xf, route, g.reshape(1, d), b.reshape(1, d), ys)


def _moe_ln(xf, rw, rb, wgu, bgu, wd, bd, g, b, layer):
    t, d = xf.shape
    route, cnt = _router(xf, rw, rb)
    cnt = cnt[0, :N_EXPERTS].astype(jnp.int32)
    tiles_e = jnp.maximum((cnt + MOE_TM - 1) // MOE_TM, 1)
    gend = jnp.cumsum(tiles_e) * MOE_TM
    gstart = gend - tiles_e * MOE_TM
    eid = route[:, ROUTE_EID:ROUTE_EID + TOP_K].astype(jnp.int32)
    rank = route[:, ROUTE_RANK:ROUTE_RANK + TOP_K].astype(jnp.int32)
    onehot = eid[..., None] == jnp.arange(N_EXPERTS, dtype=jnp.int32)
    dest = jnp.sum(jnp.where(onehot, gstart, 0), axis=-1) + rank
    dest_flat = dest.reshape(-1).astype(jnp.int32)
    n_tiles = (t * TOP_K) // MOE_TM + N_EXPERTS
    tile_row = jnp.arange(n_tiles, dtype=jnp.int32)[:, None] * MOE_TM
    tile_eid = jnp.minimum(jnp.sum((gend[None, :] <= tile_row).astype(jnp.int32), axis=1), N_EXPERTS - 1)
    nused = (gend[-1:] // MOE_TM).astype(jnp.int32)
    xs = _dispatch(xf, dest_flat, gend.astype(jnp.int32), n_tiles * MOE_TM)
    ys = _experts(xs, tile_eid, nused, wgu, bgu, wd, bd, layer)
    return _combine_ln(xf, ys, dest_flat, route, g, b)


def _compress(kv, pe, w1, b1, w2, b2, t, rope_tab):
    g = A_KV_GROUPS
    nch = t // CMP_STRIDE
    n_pad = nch
    chunks = kv.reshape(nch, CMP_STRIDE, g, HEAD_DIM)
    blocks = jnp.concatenate([chunks[:-1], chunks[1:]], axis=1)
    flat = blocks.transpose(2, 0, 1, 3).reshape(g, nch - 1, CMP_LEN * HEAD_DIM)
    flat = jnp.pad(flat, ((0, 0), (0, 1), (0, 0))).reshape(g * n_pad, CMP_LEN * HEAD_DIM)
    w1b = w1.astype(BF16)
    pe_rows = jnp.zeros((8, CMP_LEN * HEAD_DIM), F32).at[0].set(pe.reshape(-1)).astype(BF16)
    pe_term = _mm(pe_rows, w1b, out_dtype=F32, name="cmp_pe")[0]
    hid = _mm(flat, w1b, bias=pe_term + b1, act="gelu", name="cmp_mlp1")
    if rope_tab is None:
        return _mm(hid, w2.astype(BF16), bias=b2, tn=LANE, name="cmp_mlp2")
    return _mm(hid, w2.astype(BF16), bias=b2, tn=LANE, rope_tabs=rope_tab[None],
               rope_modes=jnp.ones((1,), jnp.int32), name="cmp_mlp2_rope")


def _even_w_kernel(a_ref, b_ref, o_ref, *, first_shifted, gate_tile):
    j = pl.program_id(1)
    a = a_ref[...]
    tn = a.shape[1]

    @pl.when(j < first_shifted)
    def _():
        o_ref[...] = a.astype(o_ref.dtype)

    @pl.when((j >= first_shifted) & (j < gate_tile))
    def _():
        b = b_ref[...]
        o_ref[...] = jnp.concatenate([a[:, A_GATE:], b[:, :A_GATE]], axis=1).astype(o_ref.dtype)

    @pl.when(j == gate_tile)
    def _():
        lane = lax.broadcasted_iota(jnp.int32, a.shape, 1)
        o_ref[...] = jnp.where(lane < A_GATE, a, 0.0).astype(o_ref.dtype)


def _even_w_layout(w_in, e, tr=512):
    d = w_in.shape[1]
    tr = min(tr, d)
    first_shifted = (A_Q + 6 * A_KV) // MM_TN
    gate_tile = EVEN_COLS // MM_TN - 1
    last_in = (w_in.shape[2] - 1) // MM_TN
    kern = functools.partial(_even_w_kernel, first_shifted=first_shifted, gate_tile=gate_tile)
    return pl.pallas_call(
        kern, grid=(d // tr, EVEN_COLS // MM_TN),
        in_specs=[pl.BlockSpec((None, tr, MM_TN),
                               lambda i, j: (e, i, jnp.where(j == gate_tile, first_shifted, j))),
                  pl.BlockSpec((None, tr, MM_TN), lambda i, j: (e, i, jnp.minimum(j + 1, last_in)))],
        out_specs=pl.BlockSpec((tr, MM_TN), lambda i, j: (i, j)),
        out_shape=jax.ShapeDtypeStruct((d, EVEN_COLS), BF16),
        compiler_params=_cparams(("parallel", "arbitrary")), name="even_w_layout",
    )(w_in, w_in)


def _even_mixer(xb, w_in, w_out, e, cmpk, cmpv, tabs):
    t, d = xb.shape
    w = _even_w_layout(w_in, e)
    tile_modes = np.zeros((EVEN_COLS // MM_TN,), np.int32)
    per = MM_TN // LANE
    for cb, nblk, mode in ((CB_QA, A_Q // LANE, 3), (CB_KS, A_KV // LANE, 1), (CB_KW, A_KV // LANE, 1),
                           (CB_QB, B_W // LANE, 2), (CB_KB, B_W // LANE, 1)):
        tile_modes[cb // per:(cb + nblk) // per] = mode
    h = _mm(xb, w, rope_tabs=tabs["qk"], rope_modes=jnp.asarray(tile_modes), name="even_in_proj")

    n_pad = t // CMP_STRIDE
    kc = _compress(h[:, CB_KC * LANE:CB_VC * LANE], *cmpk, t, jnp.tile(tabs["cmp"], (A_KV_GROUPS, 1)))
    vc = _compress(h[:, CB_VC * LANE:CB_KS * LANE], *cmpv, t, None)
    o_c, sel = _cmp_attn(h, kc, vc, t)
    o_s = _sel_attn(h, sel, t)

    tqw = min(WIN_LEN, t)
    (o_w,) = _swa(h, h, h, n_r=A_KV_GROUPS, n_tiles=t // tqw, tq=tqw, nh=A_HPG, shared_kv=True,
                  max_dist=WIN_LEN - 1, q_map=lambda r: r, k_map=lambda r: CB_KW + r,
                  v_map=lambda r: CB_VW + r, out_cols=A_Q, o_map=lambda r: r, with_lse=False,
                  log2_scores=True,
                  name="nsa_window")

    ob_parts, lse_parts = [], []
    for gi, (window, dil) in enumerate(B_DILATIONS):
        if dil == 1:
            tqd = min(max(window, LANE), t)
            o, lse = _swa(h, h, h, n_r=1, n_tiles=t // tqd, tq=tqd, nh=B_HPG, shared_kv=False,
                          max_dist=window, q_map=lambda r, b=CB_QB // B_HPG + gi: b,
                          k_map=lambda r, b=CB_KB // B_HPG + gi: b,
                          v_map=lambda r, b=CB_VB // B_HPG + gi: b,
                          out_cols=B_HPG * LANE, o_map=lambda r: r, with_lse=True, log2_scores=False,
                          name="dilated_1")
        else:
            o, lse = _band_attn(h, t, q_cb=CB_QB + gi * B_HPG, k_cb=CB_KB + gi * B_HPG,
                                v_cb=CB_VB + gi * B_HPG, nh=B_HPG, window=window, dil=dil)
        ob_parts.append(o)
        lse_parts.append(lse)
    o_b = jnp.concatenate(ob_parts, axis=1)
    lse_b = jnp.concatenate(lse_parts, axis=1)
    mix_in = _mixprep(o_c, o_s, o_w, h, o_b, lse_b, t)
    return _mm(mix_in, w_out, out_dtype=F32, b_layer=e, name="even_out_proj")


def _odd_mixer(xb, w_in, w_out, o_idx, lq1, lk1, lq2, lk2, sub_g, layer, tabs):
    t, d = xb.shape
    per = MM_TN // LANE
    tile_modes = np.zeros(((2 * C_QK + C_V) // MM_TN,), np.int32)
    tile_modes[:C_QK // MM_TN] = 3
    tile_modes[C_QK // MM_TN:2 * C_QK // MM_TN] = 1
    h = _mm(xb, w_in, rope_tabs=tabs["qk"], rope_modes=jnp.asarray(tile_modes), b_layer=o_idx,
            name="odd_in_proj")
    o = _diff_attn(h, lq1, lk1, lq2, lk2, sub_g, layer, t)
    return _mm(o, w_out, out_dtype=F32, b_layer=o_idx, name="odd_out_proj")


def kernel(x, even_w_in, even_w_out, cmpk_pe, cmpk_w1, cmpk_b1, cmpk_w2, cmpk_b2, cmpv_pe, cmpv_w1, cmpv_b1, cmpv_w2, cmpv_b2, odd_w_in, odd_w_out, lam_q1, lam_k1, lam_q2, lam_k2, subln_g, ln_mix_g, ln_mix_b, ln_ffn_g, ln_ffn_b, router_w, router_b, exp_w_gu, exp_b_gu, exp_w_down, exp_b_down):
    bsz, t, d = x.shape
    assert bsz == 1
    xf = x.reshape(t, d)
    xb = xf.astype(BF16)
    pos = jnp.arange(t, dtype=jnp.int32)
    cmp_end = jnp.arange(t // CMP_STRIDE, dtype=jnp.int32) * CMP_STRIDE + (CMP_LEN - 1)
    tabs = {
        "qk": jnp.stack([_rope_tables(pos, 1.0), _rope_tables(pos, HEAD_DIM ** -0.5),
                         _rope_tables(pos, HEAD_DIM ** -0.5 * math.log2(math.e))]),
        "cmp": _rope_tables(cmp_end, 1.0),
    }
    even_w_out_b = even_w_out.astype(BF16)
    odd_w_in_b = odd_w_in.astype(BF16)
    odd_w_out_b = odd_w_out.astype(BF16)
    wgu_b = exp_w_gu.astype(BF16)
    wd_b = exp_w_down.astype(BF16)
    bgu = exp_b_gu[:, :, None, :].astype(F32)
    bd = exp_b_down[:, :, None, :].astype(F32)
    for layer in range(DEPTH):
        if layer % 2 == 0:
            e = layer // 2
            mix = _even_mixer(xb, even_w_in, even_w_out_b, e,
                              (cmpk_pe[e], cmpk_w1[e], cmpk_b1[e], cmpk_w2[e], cmpk_b2[e]),
                              (cmpv_pe[e], cmpv_w1[e], cmpv_b1[e], cmpv_w2[e], cmpv_b2[e]), tabs)
        else:
            o = layer // 2
            mix = _odd_mixer(xb, odd_w_in_b, odd_w_out_b, o, lam_q1[o], lam_k1[o], lam_q2[o],
                             lam_k2[o], subln_g[o], layer, tabs)
        xf, xb = _ln_res(xf, mix, ln_mix_g[layer], ln_mix_b[layer])
        xf, xb = _moe_ln(xf, router_w[layer], router_b[layer], wgu_b, bgu, wd_b, bd,
                         ln_ffn_g[layer], ln_ffn_b[layer], layer)
    return xf.reshape(bsz, t, d)
```

```python
import functools
import math

import jax
import jax.numpy as jnp
import numpy as np
from jax import lax
from jax.experimental import pallas as pl
from jax.experimental.pallas import tpu as pltpu

F32 = jnp.float32
BF16 = jnp.bfloat16

DEPTH = 4
HEAD_DIM = 128
ROPE_THETA = 500000.0
ROT_DIM = HEAD_DIM // 4
ROT_HALF = ROT_DIM // 2
NEG_INF = -1e30
SEL_FORCE = 1e9

A_HEADS = 20
A_KV_GROUPS = 4
A_HPG = A_HEADS // A_KV_GROUPS
CMP_LEN = 32
CMP_STRIDE = 16
CMP_HIDDEN = 4 * HEAD_DIM
SLC_LEN = 64
SLC_TOPK = 16
WIN_LEN = 512

B_DILATIONS = ((128, 1), (512, 4), (2048, 16))
B_HPG = 4
B_HEADS = B_HPG * len(B_DILATIONS)

C_HEADS = 16
C_VDIM = 2 * HEAD_DIM

N_EXPERTS = 32
TOP_K = 4
D_EXPERT = 384
SWIGLU_LIMIT = 7.0
SWIGLU_ALPHA = 1.702

DEEPNORM_ALPHA = (2.0 * DEPTH) ** 0.25

A_Q = A_HEADS * HEAD_DIM
A_KV = A_KV_GROUPS * HEAD_DIM
A_GATE = A_HEADS * 3
B_W = B_HEADS * HEAD_DIM
C_QK = 2 * C_HEADS * HEAD_DIM
C_V = C_HEADS * C_VDIM

LANE = 128
VMEM_LIMIT = 56 * 1024 * 1024

GATE_PAD = 512
EVEN_COLS = A_Q + 6 * A_KV + 3 * B_W + GATE_PAD
CB_QA = 0
CB_KC = A_Q // LANE
CB_VC = CB_KC + A_KV // LANE
CB_KS = CB_VC + A_KV // LANE
CB_VS = CB_KS + A_KV // LANE
CB_KW = CB_VS + A_KV // LANE
CB_VW = CB_KW + A_KV // LANE
CB_QB = CB_VW + A_KV // LANE
CB_KB = CB_QB + B_W // LANE
CB_VB = CB_KB + B_W // LANE
CB_GATE = CB_VB + B_W // LANE

MM_TN = 512


def _cparams(sem):
    return pltpu.CompilerParams(dimension_semantics=sem, vmem_limit_bytes=VMEM_LIMIT)


def _split_top16(a):
    bits = lax.bitcast_convert_type(a, jnp.uint32) & jnp.uint32(0xFFFF0000)
    hi = lax.bitcast_convert_type(bits, F32)
    return hi, a - hi


def _nt_dot(a, b):
    return lax.dot_general(a, b, (((1,), (1,)), ((), ())), preferred_element_type=F32)


def _rope_lanes(x, tab):
    c = tab[:, 0:LANE]
    sa = tab[:, LANE:2 * LANE]
    sb = tab[:, 2 * LANE:3 * LANE]
    return (x * c + pltpu.roll(x, LANE - ROT_HALF, axis=1) * sa
            + pltpu.roll(x, ROT_HALF, axis=1) * sb)


def _mm_kernel(*refs, has_bias, act, has_rope, n_sub):
    if has_rope:
        modes_ref, a_ref, b_ref = refs[0], refs[1], refs[2]
        rest = refs[3:]
    else:
        a_ref, b_ref = refs[0], refs[1]
        rest = refs[2:]
    idx = 0
    if has_bias:
        bias_ref = rest[idx]
        idx += 1
    if has_rope:
        tab_ref = rest[idx]
        idx += 1
    o_ref = rest[idx]

    acc = jnp.dot(a_ref[...], b_ref[...], preferred_element_type=F32)
    if has_bias:
        acc = acc + bias_ref[...]
    if act == "gelu":
        acc = jax.nn.gelu(acc, approximate=True)
    if not has_rope:
        o_ref[...] = acc.astype(o_ref.dtype)
        return
    mode = modes_ref[pl.program_id(1)]

    @pl.when(mode == 0)
    def _():
        o_ref[...] = acc.astype(o_ref.dtype)

    @pl.when(mode != 0)
    def _():
        tab = tab_ref[...]
        for s in range(n_sub):
            sl = slice(s * LANE, (s + 1) * LANE)
            o_ref[:, sl] = _rope_lanes(acc[:, sl], tab).astype(o_ref.dtype)


def _mm(a, b, *, bias=None, act=None, rope_tabs=None, rope_modes=None,
        out_dtype=None, tm=1024, tn=MM_TN, b_layer=None, name="mm"):
    out_dtype = BF16 if out_dtype is None else out_dtype
    m, k = a.shape
    k2, n = b.shape[-2:]
    assert k == k2 and (b.ndim == 3) == (b_layer is not None)
    tm = min(tm, m)
    tn = min(tn, n)
    assert m % tm == 0 and n % tn == 0
    has_bias = bias is not None
    has_rope = rope_tabs is not None
    grid = (m // tm, n // tn)
    kern = functools.partial(_mm_kernel, has_bias=has_bias, act=act, has_rope=has_rope,
                             n_sub=tn // LANE)
    if b_layer is None:
        b_spec = pl.BlockSpec((k, tn), lambda i, j, *_: (0, j))
    else:
        b_spec = pl.BlockSpec((None, k, tn), lambda i, j, *_: (b_layer, 0, j))
    if has_rope:
        in_specs = [pl.BlockSpec((tm, k), lambda i, j, md: (i, 0)), b_spec]
        args = [a, b]
        if has_bias:
            in_specs.append(pl.BlockSpec((1, tn), lambda i, j, md: (0, j)))
            args.append(bias.reshape(1, n).astype(F32))
        in_specs.append(pl.BlockSpec((None, tm, 3 * LANE),
                                     lambda i, j, md: (jnp.maximum(md[j] - 1, 0), i, 0)))
        args.append(rope_tabs)
        gs = pltpu.PrefetchScalarGridSpec(
            num_scalar_prefetch=1, grid=grid, in_specs=in_specs,
            out_specs=pl.BlockSpec((tm, tn), lambda i, j, md: (i, j)))
        return pl.pallas_call(kern, grid_spec=gs,
                              out_shape=jax.ShapeDtypeStruct((m, n), out_dtype),
                              compiler_params=_cparams(("parallel", "arbitrary")),
                              name=name)(rope_modes, *args)
    in_specs = [pl.BlockSpec((tm, k), lambda i, j: (i, 0)), b_spec]
    args = [a, b]
    if has_bias:
        in_specs.append(pl.BlockSpec((1, tn), lambda i, j: (0, j)))
        args.append(bias.reshape(1, n).astype(F32))
    return pl.pallas_call(kern, grid=grid, in_specs=in_specs,
                          out_specs=pl.BlockSpec((tm, tn), lambda i, j: (i, j)),
                          out_shape=jax.ShapeDtypeStruct((m, n), out_dtype),
                          compiler_params=_cparams(("parallel", "arbitrary")),
                          name=name)(*args)


def _ln_kernel(x_ref, y_ref, g_ref, b_ref, of_ref, ob_ref):
    z = DEEPNORM_ALPHA * x_ref[...] + y_ref[...].astype(F32)
    mu = jnp.mean(z, axis=-1, keepdims=True)
    zc = z - mu
    var = jnp.mean(zc * zc, axis=-1, keepdims=True)
    out = zc * lax.rsqrt(var + 1e-5) * g_ref[...] + b_ref[...]
    of_ref[...] = out
    ob_ref[...] = out.astype(BF16)


def _ln_res(x, y, g, b, tm=256):
    t, d = x.shape
    tm = min(tm, t)
    row = pl.BlockSpec((tm, d), lambda i: (i, 0))
    vec = pl.BlockSpec((1, d), lambda i: (0, 0))
    return pl.pallas_call(
        _ln_kernel, grid=(t // tm,), in_specs=[row, row, vec, vec], out_specs=[row, row],
        out_shape=[jax.ShapeDtypeStruct((t, d), F32), jax.ShapeDtypeStruct((t, d), BF16)],
        compiler_params=_cparams(("parallel",)), name="ln_res",
    )(x, y, g.reshape(1, d), b.reshape(1, d))


def _rope_tables(pos, scale):
    inv = ROPE_THETA ** (-jnp.arange(ROT_HALF, dtype=F32) / ROT_HALF)
    ang = pos.astype(F32)[:, None] * inv[None, :]
    cos = jnp.cos(ang)
    sin = jnp.sin(ang)
    n = pos.shape[0]
    ones = jnp.ones((n, HEAD_DIM - ROT_DIM), F32)
    zer = jnp.zeros((n, HEAD_DIM - ROT_HALF), F32)
    c = jnp.concatenate([cos, cos, ones], axis=1)
    sa = jnp.concatenate([-sin, zer], axis=1)
    sb = jnp.concatenate([jnp.zeros((n, ROT_HALF), F32), sin,
                          jnp.zeros((n, HEAD_DIM - ROT_DIM), F32)], axis=1)
    return jnp.concatenate([c, sa, sb], axis=1) * scale


def _swa_kernel(q_ref, kp_ref, kc_ref, vp_ref, vc_ref, *out_refs, nh, shared_kv, max_dist,
                tq, with_lse, log2_scores):
    assert not (with_lse and log2_scores)
    o_ref = out_refs[0]
    i = pl.program_id(1)
    row = lax.broadcasted_iota(jnp.int32, (tq, 2 * tq), 0)
    col = lax.broadcasted_iota(jnp.int32, (tq, 2 * tq), 1)
    dist = row + tq - col
    mask = (dist >= 0) & (dist <= max_dist) & (col + (i - 1) * tq >= 0)
    for h in range(nh):
        kv = 0 if shared_kv else h
        hs = slice(h * LANE, (h + 1) * LANE)
        ks = slice(kv * LANE, (kv + 1) * LANE)
        q = q_ref[:, hs]
        k = jnp.concatenate([kp_ref[:, ks], kc_ref[:, ks]], axis=0)
        v = jnp.concatenate([vp_ref[:, ks], vc_ref[:, ks]], axis=0)
        s = jnp.where(mask, _nt_dot(q, k), NEG_INF)
        m = jnp.max(s, axis=-1, keepdims=True)
        e = jnp.exp2(s - m) if log2_scores else jnp.exp(s - m)
        den = jnp.sum(e, axis=-1, keepdims=True)
        o = jnp.dot(e.astype(BF16), v, preferred_element_type=F32) / den
        o_ref[:, hs] = o.astype(o_ref.dtype)
        if with_lse:
            out_refs[1][:, hs] = jnp.broadcast_to(m + jnp.log(den), (tq, LANE))


def _swa(qsrc, ksrc, vsrc, *, n_r, n_tiles, tq, nh, shared_kv, max_dist, q_map, k_map, v_map,
         out_cols, o_map, with_lse, log2_scores, name):
    kvw = LANE if shared_kv else nh * LANE
    qw = nh * LANE
    prev = lambda f: (lambda r, i: (jnp.maximum(i - 1, 0), f(r)))
    cur = lambda f: (lambda r, i: (i, f(r)))
    in_specs = [pl.BlockSpec((tq, qw), cur(q_map)),
                pl.BlockSpec((tq, kvw), prev(k_map)), pl.BlockSpec((tq, kvw), cur(k_map)),
                pl.BlockSpec((tq, kvw), prev(v_map)), pl.BlockSpec((tq, kvw), cur(v_map))]
    rows = n_tiles * tq
    out_shape = [jax.ShapeDtypeStruct((rows, out_cols), BF16)]
    out_specs = [pl.BlockSpec((tq, qw), cur(o_map))]
    if with_lse:
        out_shape.append(jax.ShapeDtypeStruct((rows, out_cols), F32))
        out_specs.append(pl.BlockSpec((tq, qw), cur(o_map)))
    kern = functools.partial(_swa_kernel, nh=nh, shared_kv=shared_kv, max_dist=max_dist, tq=tq,
                             with_lse=with_lse, log2_scores=log2_scores)
    return pl.pallas_call(kern, grid=(n_r, n_tiles), in_specs=in_specs, out_specs=out_specs,
                          out_shape=out_shape,
                          compiler_params=_cparams(("parallel", "arbitrary")),
                          name=name)(qsrc, ksrc, ksrc, vsrc, vsrc)


def _band_kernel(q_ref, k_ref, v_ref, o_ref, lse_ref, *, tq, band, window, dil):
    i = pl.program_id(1)
    start = pl.multiple_of(jnp.maximum((i + 1) * tq - band, 0), tq)
    k = k_ref[pl.ds(start, band), :]
    v = v_ref[pl.ds(start, band), :]
    dist = (lax.broadcasted_iota(jnp.int32, (tq, band), 0)
            - lax.broadcasted_iota(jnp.int32, (tq, band), 1)) + (i * tq - start)
    valid = (dist >= 0) & (dist <= window) & ((dist & (dil - 1)) == 0)
    s = jnp.where(valid, _nt_dot(q_ref[...], k), NEG_INF)
    m = jnp.max(s, axis=-1, keepdims=True)
    e = jnp.exp(s - m)
    l = jnp.sum(e, axis=-1, keepdims=True)
    o_ref[...] = (jnp.dot(e.astype(BF16), v, preferred_element_type=F32) / l).astype(o_ref.dtype)
    lse_ref[...] = jnp.broadcast_to(m + jnp.log(l), (tq, LANE))


def _band_attn(h, t, *, q_cb, k_cb, v_cb, nh, window, dil, tq=256):
    tq = min(tq, t)
    assert dil & (dil - 1) == 0 and window % tq == 0
    band = min(window + tq, t)
    kern = functools.partial(_band_kernel, tq=tq, band=band, window=window, dil=dil)
    blk = pl.BlockSpec((tq, LANE), lambda hd, i: (i, hd))
    return pl.pallas_call(
        kern, grid=(nh, t // tq),
        in_specs=[pl.BlockSpec((tq, LANE), lambda hd, i: (i, q_cb + hd)),
                  pl.BlockSpec((t, LANE), lambda hd, i: (0, k_cb + hd)),
                  pl.BlockSpec((t, LANE), lambda hd, i: (0, v_cb + hd))],
        out_specs=[blk, blk],
        out_shape=[jax.ShapeDtypeStruct((t, nh * LANE), BF16), jax.ShapeDtypeStruct((t, nh * LANE), F32)],
        compiler_params=_cparams(("parallel", "arbitrary")), name="dilated_band_%d" % dil,
    )(h, h, h)


def _cmp_kernel(q_ref, kc_ref, vc_ref, ov_ref, oc_ref, sel_ref, *, tq, n_cmp_pad, n_slc):
    i = pl.program_id(1)
    tqv = i * tq + lax.broadcasted_iota(jnp.int32, (tq, 1), 0)
    cmp_end = CMP_STRIDE * lax.broadcasted_iota(jnp.int32, (1, n_cmp_pad), 1) + (CMP_LEN - 1)
    maskc = cmp_end <= tqv
    kc = kc_ref[...]
    vc = vc_ref[...]
    psum = jnp.zeros((tq, n_cmp_pad), F32)
    for h in range(A_HPG):
        hs = slice(h * LANE, (h + 1) * LANE)
        s = jnp.where(maskc, _nt_dot(q_ref[:, hs], kc), NEG_INF)
        m = jnp.max(s, axis=-1, keepdims=True)
        e = jnp.where(maskc, jnp.exp2(s - m), 0.0)
        den = jnp.sum(e, axis=-1, keepdims=True)
        p = e / jnp.maximum(den, 1e-30)
        oc_ref[:, hs] = jnp.dot(p.astype(BF16), vc, preferred_element_type=F32).astype(oc_ref.dtype)
        psum = psum + p
    ov = ov_ref[...]
    p1, r1 = _split_top16(psum)
    p2, r2 = _split_top16(r1)
    imp = (jnp.dot(p1.astype(BF16), ov, preferred_element_type=F32)
           + jnp.dot(p2.astype(BF16), ov, preferred_element_type=F32)
           + jnp.dot(r2.astype(BF16), ov, preferred_element_type=F32))
    j = lax.broadcasted_iota(jnp.int32, (1, n_slc), 1)
    jf = j.astype(F32)
    tb = tqv // SLC_LEN
    forced = (j == 0) | (j == tb) | (j == tb - 1)
    val = jnp.where(forced, SEL_FORCE, jnp.where(j <= tb, imp, -SEL_FORCE))
    sel = jnp.zeros((tq, n_slc), F32)
    for _ in range(min(SLC_TOPK, n_slc)):
        m = jnp.max(val, axis=-1, keepdims=True)
        first = jnp.min(jnp.where(val == m, jf, float(n_slc)), axis=-1, keepdims=True)
        hit = jf == first
        sel = jnp.where(hit & (m > -0.5 * SEL_FORCE), 1.0, sel)
        val = jnp.where(hit, -3e38, val)
    sel_ref[...] = sel.astype(sel_ref.dtype)


def _cmp_attn(h, kc, vc, t, tq=512):
    n_cmp_pad = kc.shape[0] // A_KV_GROUPS
    n_slc = t // SLC_LEN
    ci = np.arange(n_cmp_pad)[:, None] * CMP_STRIDE
    sj = np.arange(n_slc)[None, :] * SLC_LEN
    overlap = ((ci < sj + SLC_LEN) & (ci + CMP_LEN > sj) & (np.arange(n_cmp_pad)[:, None] < t // CMP_STRIDE - 1))
    overlap = jnp.asarray(overlap.astype(np.float32), BF16)
    kern = functools.partial(_cmp_kernel, tq=tq, n_cmp_pad=n_cmp_pad, n_slc=n_slc)
    qw = A_HPG * LANE
    return pl.pallas_call(
        kern, grid=(A_KV_GROUPS, t // tq),
        in_specs=[pl.BlockSpec((tq, qw), lambda g, i: (i, g)),
                  pl.BlockSpec((n_cmp_pad, LANE), lambda g, i: (g, 0)),
                  pl.BlockSpec((n_cmp_pad, LANE), lambda g, i: (g, 0)),
                  pl.BlockSpec((n_cmp_pad, n_slc), lambda g, i: (0, 0))],
        out_specs=[pl.BlockSpec((tq, qw), lambda g, i: (i, g)),
                   pl.BlockSpec((tq, n_slc), lambda g, i: (i, g))],
        out_shape=[jax.ShapeDtypeStruct((t, A_Q), BF16),
                   jax.ShapeDtypeStruct((t, A_KV_GROUPS * n_slc), BF16)],
        compiler_params=_cparams(("parallel", "arbitrary")), name="cmp_attn_topk",
    )(h, kc, vc, overlap)


ROW_CHUNK = 32


def _flash_pipeline(n, score_stage, softmax_stage, value_stage):
    @pl.when(n > 0)
    def _():
        score_stage(0, 0)

    def step(t, slot):
        softmax_stage(slot)
        value_stage(1 - slot, jnp.maximum(t - 1, 0))
        score_stage(1 - slot, jnp.minimum(t + 1, n - 1))

    def pair(u, c):
        step(2 * u, 0)
        step(2 * u + 1, 1)
        return c

    lax.fori_loop(0, n // 2, pair, 0)
    last = jnp.maximum(n - 1, 0)

    @pl.when(n % 2 == 1)
    def _():
        step(n - 1, 0)
        value_stage(0, last)

    @pl.when(n % 2 == 0)
    def _():
        value_stage(1, last)


def _sel_kernel(q_ref, k_ref, v_ref, sel_ref, o_ref, q5_sc, s_sc, bias_sc, p_sc, acc_sc, m_sc, l_sc,
                alpha_sc, *, tq, tk, n_slc):
    i = pl.program_id(1)
    nh = A_HPG
    for h in range(nh):
        q5_sc[h * tq:(h + 1) * tq, :] = q_ref[:, h * LANE:(h + 1) * LANE]
    m_sc[...] = jnp.full(m_sc.shape, NEG_INF, F32)
    l_sc[...] = jnp.zeros_like(l_sc)
    acc_sc[...] = jnp.zeros_like(acc_sc)
    p_sc[1] = jnp.zeros(p_sc.shape[1:], p_sc.dtype)
    alpha_sc[1] = jnp.ones(alpha_sc.shape[1:], F32)
    selb = sel_ref[...]
    tqv = i * tq + lax.broadcasted_iota(jnp.int32, (tq, 1), 0)
    blk_per_tile = tk // SLC_LEN
    blk_gap = (lax.broadcasted_iota(jnp.int32, (n_slc, tk), 0)
               - lax.broadcasted_iota(jnp.int32, (n_slc, tk), 1) // SLC_LEN)
    kcol = lax.broadcasted_iota(jnp.int32, (1, tk), 1)
    n_kv = (i * tq + tq + tk - 1) // tk
    nrep = tk // LANE

    def score_stage(slot, t):
        k = k_ref[pl.ds(pl.multiple_of(t * tk, tk), tk), :]
        s_sc[slot] = _nt_dot(q5_sc[...], k)
        expand = jnp.where(blk_gap == t * blk_per_tile, 1.0, 0.0).astype(BF16)
        picked = jnp.dot(selb, expand, preferred_element_type=F32)
        bias_sc[slot] = jnp.where((picked > 0.5) & (kcol + t * tk <= tqv), 0.0, NEG_INF)

    def softmax_stage(slot):
        for c in range(nh * tq // ROW_CHUNK):
            rows = slice(c * ROW_CHUNK, (c + 1) * ROW_CHUNK)
            r0 = (c * ROW_CHUNK) % tq
            s = s_sc[slot, rows, :] + bias_sc[slot, r0:r0 + ROW_CHUNK, :]
            m_old = m_sc[rows, :]
            m_new = jnp.maximum(m_old, jnp.max(s, axis=-1, keepdims=True))
            alpha = jnp.exp2(m_old - m_new)
            p = jnp.exp2(s - jnp.concatenate([m_new] * nrep, axis=1))
            l_sc[rows, :] = alpha * l_sc[rows, :] + jnp.sum(p, axis=-1, keepdims=True)
            m_sc[rows, :] = m_new
            alpha_sc[slot, rows, :] = alpha
            p_sc[slot, rows, :] = p.astype(p_sc.dtype)

    def value_stage(slot, t):
        v = v_ref[pl.ds(pl.multiple_of(t * tk, tk), tk), :]
        acc_sc[...] = acc_sc[...] * alpha_sc[slot] + jnp.dot(p_sc[slot], v, preferred_element_type=F32)

    _flash_pipeline(n_kv, score_stage, softmax_stage, value_stage)
    o = acc_sc[...] / l_sc[...]
    for h in range(nh):
        o_ref[:, h * LANE:(h + 1) * LANE] = o[h * tq:(h + 1) * tq].astype(o_ref.dtype)


def _sel_attn(h, sel, t, tq=128, tk=512):
    n_slc = t // SLC_LEN
    tk = min(tk, t)
    qw = A_HPG * LANE
    rows = A_HPG * tq
    kern = functools.partial(_sel_kernel, tq=tq, tk=tk, n_slc=n_slc)
    return pl.pallas_call(
        kern, grid=(A_KV_GROUPS, t // tq),
        in_specs=[pl.BlockSpec((tq, qw), lambda g, i: (i, g)),
                  pl.BlockSpec((t, LANE), lambda g, i: (0, CB_KS + g)),
                  pl.BlockSpec((t, LANE), lambda g, i: (0, CB_VS + g)),
                  pl.BlockSpec((tq, n_slc), lambda g, i: (i, g))],
        out_specs=pl.BlockSpec((tq, qw), lambda g, i: (i, g)),
        out_shape=jax.ShapeDtypeStruct((t, A_Q), BF16),
        scratch_shapes=[pltpu.VMEM((rows, LANE), BF16), pltpu.VMEM((2, rows, tk), F32),
                        pltpu.VMEM((2, tq, tk), F32), pltpu.VMEM((2, rows, tk), BF16),
                        pltpu.VMEM((rows, LANE), F32), pltpu.VMEM((rows, LANE), F32),
                        pltpu.VMEM((rows, LANE), F32), pltpu.VMEM((2, rows, LANE), F32)],
        compiler_params=_cparams(("parallel", "arbitrary")), name="sel_attn",
    )(h, h, h, sel)


def _mixprep_kernel(oc_ref, os_ref, ow_ref, gate_ref, ob_ref, lse_ref, out_ref):
    gate = jax.nn.sigmoid(gate_ref[...].astype(F32))
    for h in range(A_HEADS):
        hs = slice(h * LANE, (h + 1) * LANE)
        o = (gate[:, 3 * h:3 * h + 1] * oc_ref[:, hs].astype(F32)
             + gate[:, 3 * h + 1:3 * h + 2] * os_ref[:, hs].astype(F32)
             + gate[:, 3 * h + 2:3 * h + 3] * ow_ref[:, hs].astype(F32))
        out_ref[:, hs] = o.astype(out_ref.dtype)
    ng = len(B_DILATIONS)
    for hi in range(B_HPG):
        lses = [lse_ref[:, (g * B_HPG + hi) * LANE:(g * B_HPG + hi + 1) * LANE] for g in range(ng)]
        m = functools.reduce(jnp.maximum, lses)
        es = [jnp.exp(x - m) for x in lses]
        den = functools.reduce(lambda a, b: a + b, es)
        for g in range(ng):
            src = slice((g * B_HPG + hi) * LANE, (g * B_HPG + hi + 1) * LANE)
            dst = slice(A_Q + (g * B_HPG + hi) * LANE, A_Q + (g * B_HPG + hi + 1) * LANE)
            out_ref[:, dst] = (ob_ref[:, src].astype(F32) * (es[g] / den)).astype(out_ref.dtype)


def _mixprep(o_c, o_s, o_w, h, o_b, lse_b, t, tq=256):
    tq = min(tq, t)
    a = pl.BlockSpec((tq, A_Q), lambda i: (i, 0))
    b = pl.BlockSpec((tq, B_W), lambda i: (i, 0))
    return pl.pallas_call(
        _mixprep_kernel, grid=(t // tq,),
        in_specs=[a, a, a, pl.BlockSpec((tq, LANE), lambda i: (i, CB_GATE)), b, b],
        out_specs=pl.BlockSpec((tq, A_Q + B_W), lambda i: (i, 0)),
        out_shape=jax.ShapeDtypeStruct((t, A_Q + B_W), BF16),
        compiler_params=_cparams(("parallel",)), name="even_mixprep",
    )(o_c, o_s, o_w, h, o_b, lse_b)


def _diff_kernel(q_ref, k_ref, v_ref, lq1_ref, lk1_ref, lq2_ref, lk2_ref, g_ref, o_ref,
                 s_sc, p_sc, acc_sc, m_sc, l_sc, alpha_sc, *, tq, tk, lam_init):
    i = pl.program_id(1)
    q1 = q_ref[:, 0:LANE]
    q2 = q_ref[:, LANE:2 * LANE]
    lam = (jnp.exp(jnp.sum(lq1_ref[...] * lk1_ref[...], axis=-1, keepdims=True))
           - jnp.exp(jnp.sum(lq2_ref[...] * lk2_ref[...], axis=-1, keepdims=True)) + lam_init)

    m_sc[...] = jnp.full(m_sc.shape, NEG_INF, F32)
    l_sc[...] = jnp.zeros_like(l_sc)
    acc_sc[...] = jnp.zeros_like(acc_sc)
    p_sc[1] = jnp.zeros(p_sc.shape[1:], p_sc.dtype)
    alpha_sc[1] = jnp.ones(alpha_sc.shape[1:], F32)
    nrep = tk // LANE
    rep = C_VDIM // LANE

    def score_stage(slot, start):
        k = k_ref[pl.ds(start, tk), :]
        s_sc[slot, 0] = _nt_dot(q1, k[:, 0:LANE])
        s_sc[slot, 1] = _nt_dot(q2, k[:, LANE:2 * LANE])

    def softmax_stage(slot, diag_off):
        for hd in range(2):
            for c in range(tq // ROW_CHUNK):
                rows = slice(c * ROW_CHUNK, (c + 1) * ROW_CHUNK)
                s = s_sc[slot, hd, rows, :]
                if diag_off is not None:
                    row = lax.broadcasted_iota(jnp.int32, (ROW_CHUNK, tk), 0) + c * ROW_CHUNK
                    col = lax.broadcasted_iota(jnp.int32, (ROW_CHUNK, tk), 1) + diag_off
                    s = jnp.where(col <= row, s, NEG_INF)
                m_old = m_sc[hd, rows, :]
                m_new = jnp.maximum(m_old, jnp.max(s, axis=-1, keepdims=True))
                alpha = jnp.exp2(m_old - m_new)
                p = jnp.exp2(s - jnp.concatenate([m_new] * nrep, axis=1))
                l_sc[hd, rows, :] = alpha * l_sc[hd, rows, :] + jnp.sum(p, axis=-1, keepdims=True)
                m_sc[hd, rows, :] = m_new
                alpha_sc[slot, hd, rows, :] = alpha
                p_sc[slot, hd, rows, :] = p.astype(p_sc.dtype)

    def value_stage(slot, start):
        v = v_ref[pl.ds(start, tk), :]
        for hd in range(2):
            pv = jnp.dot(p_sc[slot, hd], v, preferred_element_type=F32)
            alpha = alpha_sc[slot, hd]
            acc_sc[hd] = acc_sc[hd] * jnp.concatenate([alpha] * rep, axis=1) + pv

    per = tq // tk
    n_full = i * per
    _flash_pipeline(n_full, lambda slot, t: score_stage(slot, pl.multiple_of(t * tk, tk)),
                    lambda slot: softmax_stage(slot, None),
                    lambda slot, t: value_stage(slot, pl.multiple_of(t * tk, tk)))
    for u in range(per):
        start = pl.multiple_of(i * tq + u * tk, tk)
        score_stage(0, start)
        softmax_stage(0, u * tk)
        value_stage(0, start)
    l1 = jnp.concatenate([l_sc[0]] * rep, axis=1)
    l2 = jnp.concatenate([l_sc[1]] * rep, axis=1)
    o = acc_sc[0] / l1 - lam * (acc_sc[1] / l2)
    o = o * lax.rsqrt(jnp.mean(o * o, axis=-1, keepdims=True) + 1e-5) * g_ref[...] * (1.0 - lam_init)
    o_ref[...] = o.astype(o_ref.dtype)


def _diff_attn(h, lq1, lk1, lq2, lk2, sub_g, layer, t, tq=512, tk=512):
    tq = min(tq, t)
    tk = min(tk, tq)
    lam_init = 0.8 - 0.6 * math.exp(-0.3 * layer)
    kern = functools.partial(_diff_kernel, tq=tq, tk=tk, lam_init=lam_init)
    vec = pl.BlockSpec((1, LANE), lambda hh, i: (0, 0))
    qkb = C_QK // C_VDIM
    return pl.pallas_call(
        kern, grid=(C_HEADS, t // tq),
        in_specs=[pl.BlockSpec((tq, C_VDIM), lambda hh, i: (i, hh)),
                  pl.BlockSpec((t, C_VDIM), lambda hh, i: (0, qkb + hh)),
                  pl.BlockSpec((t, C_VDIM), lambda hh, i: (0, 2 * qkb + hh)),
                  vec, vec, vec, vec, pl.BlockSpec((1, C_VDIM), lambda hh, i: (0, 0))],
        out_specs=pl.BlockSpec((tq, C_VDIM), lambda hh, i: (i, hh)),
        out_shape=jax.ShapeDtypeStruct((t, C_V), BF16),
        scratch_shapes=[pltpu.VMEM((2, 2, tq, tk), F32), pltpu.VMEM((2, 2, tq, tk), BF16),
                        pltpu.VMEM((2, tq, C_VDIM), F32), pltpu.VMEM((2, tq, LANE), F32),
                        pltpu.VMEM((2, tq, LANE), F32), pltpu.VMEM((2, 2, tq, LANE), F32)],
        compiler_params=_cparams(("parallel", "arbitrary")), name="diff_attn",
    )(h, h, h, lq1.reshape(1, LANE).astype(F32), lk1.reshape(1, LANE).astype(F32),
      lq2.reshape(1, LANE).astype(F32), lk2.reshape(1, LANE).astype(F32),
      sub_g.reshape(1, C_VDIM).astype(F32))


ROUTE_EID = 0
ROUTE_RANK = 4
ROUTE_W = 8
MOE_TM = 256


def _router_kernel(x_ref, whi_ref, wlo_ref, b_ref, route_ref, cnt_ref, carry_ref):
    i = pl.program_id(0)

    @pl.when(i == 0)
    def _():
        carry_ref[...] = jnp.zeros_like(carry_ref)

    x_hi, x_lo = _split_top16(x_ref[...])
    x_hi = x_hi.astype(BF16)
    x_lo = x_lo.astype(BF16)
    logits = (jnp.dot(x_hi, whi_ref[...], preferred_element_type=F32)
              + jnp.dot(x_lo, whi_ref[...], preferred_element_type=F32)
              + jnp.dot(x_hi, wlo_ref[...], preferred_element_type=F32)
              + jnp.dot(x_lo, wlo_ref[...], preferred_element_type=F32)) + b_ref[...]
    tm, ne = logits.shape
    lane = lax.broadcasted_iota(jnp.int32, (1, ne), 1)
    jf = lane.astype(F32)
    val = logits
    tops, hits, firsts = [], [], []
    for _ in range(TOP_K):
        m = jnp.max(val, axis=-1, keepdims=True)
        first = jnp.min(jnp.where(val == m, jf, float(ne)), axis=-1, keepdims=True)
        hit = jf == first
        tops.append(m)
        hits.append(hit)
        firsts.append(first)
        val = jnp.where(hit, -3e38, val)
    es = [jnp.exp(x - tops[0]) for x in tops]
    den = functools.reduce(lambda a, b: a + b, es)
    picked = jnp.zeros((tm, ne), F32)
    for hit in hits:
        picked = jnp.where(hit, 1.0, picked)
    row = lax.broadcasted_iota(jnp.int32, (tm, tm), 0)
    col = lax.broadcasted_iota(jnp.int32, (tm, tm), 1)
    tri = jnp.where(row > col, 1.0, 0.0).astype(BF16)
    rank = jnp.dot(tri, picked.astype(BF16), preferred_element_type=F32) + carry_ref[...]
    carry_ref[...] += jnp.sum(picked, axis=0, keepdims=True)
    cnt_ref[...] = carry_ref[...]
    route = jnp.zeros((tm, ne), F32)
    for k in range(TOP_K):
        rk = jnp.sum(jnp.where(hits[k], rank, 0.0), axis=-1, keepdims=True)
        route = jnp.where(lane == ROUTE_EID + k, firsts[k], route)
        route = jnp.where(lane == ROUTE_RANK + k, rk, route)
        route = jnp.where(lane == ROUTE_W + k, es[k] / den, route)
    route_ref[...] = route


def _router(x, rw, rb, tm=256):
    t, d = x.shape
    tm = min(tm, t)
    rw_p = jnp.pad(rw.astype(F32), ((0, 0), (0, LANE - N_EXPERTS)))
    rb_p = jnp.pad(rb.astype(F32), (0, LANE - N_EXPERTS), constant_values=NEG_INF).reshape(1, LANE)
    w_hi, w_lo = _split_top16(rw_p)
    w_hi = w_hi.astype(BF16)
    w_lo = w_lo.astype(BF16)
    wblk = pl.BlockSpec((d, LANE), lambda i: (0, 0))
    return pl.pallas_call(
        _router_kernel, grid=(t // tm,),
        in_specs=[pl.BlockSpec((tm, d), lambda i: (i, 0)), wblk, wblk,
                  pl.BlockSpec((1, LANE), lambda i: (0, 0))],
        out_specs=[pl.BlockSpec((tm, LANE), lambda i: (i, 0)), pl.BlockSpec((1, LANE), lambda i: (0, 0))],
        out_shape=[jax.ShapeDtypeStruct((t, LANE), F32), jax.ShapeDtypeStruct((1, LANE), F32)],
        scratch_shapes=[pltpu.VMEM((1, LANE), F32)],
        compiler_params=_cparams(("arbitrary",)), name="moe_router",
    )(x, w_hi, w_lo, rb_p)


def _row_copy(src, src_row, dst, dst_row, sem):
    return pltpu.make_async_copy(src.at[pl.ds(src_row, 1)], dst.at[pl.ds(dst_row, 1)], sem)


def _dispatch_kernel(gend_ref, dest_ref, x_ref, xs_hbm, zeros_ref, sem, *, tm):
    i = pl.program_id(0)

    @pl.when(i == 0)
    def _():
        zeros_ref[...] = jnp.zeros_like(zeros_ref)

        def clear(e):
            start = pl.multiple_of(gend_ref[e] - MOE_TM, MOE_TM)
            return pltpu.make_async_copy(zeros_ref, xs_hbm.at[pl.ds(start, MOE_TM)], sem)
        for e in range(N_EXPERTS):
            clear(e).start()
        for e in range(N_EXPERTS):
            clear(e).wait()

        def clear_tail(j, c):
            cp = pltpu.make_async_copy(
                zeros_ref, xs_hbm.at[pl.ds(pl.multiple_of(j * MOE_TM, MOE_TM), MOE_TM)], sem)
            cp.start()
            cp.wait()
            return c

        lax.fori_loop(gend_ref[N_EXPERTS - 1] // MOE_TM, xs_hbm.shape[0] // MOE_TM, clear_tail, 0)

    def issue(r, c):
        for k in range(TOP_K):
            _row_copy(x_ref, r, xs_hbm, dest_ref[TOP_K * r + k], sem).start(priority=k % 2)
        return c

    lax.fori_loop(0, tm, issue, 0)

    def drain(r, c):
        for k in range(TOP_K):
            _row_copy(x_ref, 0, xs_hbm, 0, sem).wait()
        return c

    lax.fori_loop(0, tm, drain, 0)


def _dispatch(xf, dest_flat, gend, n_rows, tm=256):
    t, d = xf.shape
    tm = min(tm, t)
    gs = pltpu.PrefetchScalarGridSpec(
        num_scalar_prefetch=1, grid=(t // tm,),
        in_specs=[pl.BlockSpec((TOP_K * tm,), lambda i, ge: (i,), memory_space=pltpu.SMEM),
                  pl.BlockSpec((tm, d), lambda i, ge: (i, 0))],
        out_specs=pl.BlockSpec(memory_space=pl.ANY),
        scratch_shapes=[pltpu.VMEM((MOE_TM, d), F32), pltpu.SemaphoreType.DMA(())])
    return pl.pallas_call(
        functools.partial(_dispatch_kernel, tm=tm), grid_spec=gs, out_shape=jax.ShapeDtypeStruct((n_rows, d), F32),
        compiler_params=_cparams(("arbitrary",)), name="moe_dispatch",
    )(gend, dest_flat, xf)


def _expert_kernel(eid_ref, nused_ref, xs_ref, wgu_ref, bgu_ref, wd_ref, bd_ref, pick_ref, ys_ref):
    j = pl.program_id(0)

    @pl.when(j < nused_ref[0])
    def _():
        x = xs_ref[...].astype(BF16)
        hgu = jnp.dot(x, wgu_ref[...], preferred_element_type=F32) + bgu_ref[...]
        glu = jnp.minimum(hgu, SWIGLU_LIMIT)
        gated = glu * jax.nn.sigmoid(SWIGLU_ALPHA * glu)
        lin = jnp.clip(hgu, -SWIGLU_LIMIT, SWIGLU_LIMIT) + 1.0
        prod = jnp.concatenate(
            [gated[:, s:s + LANE] * pltpu.roll(lin[:, s:s + LANE], LANE - 1, axis=1)
             for s in range(0, 2 * D_EXPERT, LANE)], axis=1)
        act = jnp.dot(prod.astype(BF16), pick_ref[...], preferred_element_type=F32)
        ys_ref[...] = jnp.dot(act.astype(BF16), wd_ref[...], preferred_element_type=F32) + bd_ref[...]

    @pl.when(j >= nused_ref[0])
    def _():
        ys_ref[...] = jnp.zeros_like(ys_ref)


def _experts(xs, tile_eid, nused, wgu, bgu, wd, bd, layer):
    n_rows, d = xs.shape
    n_tiles = n_rows // MOE_TM
    pick = np.zeros((2 * D_EXPERT, D_EXPERT), np.float32)
    pick[2 * np.arange(D_EXPERT), np.arange(D_EXPERT)] = 1.0
    tile = lambda j, eid, nu: (jnp.minimum(j, nu[0] - 1), 0)
    per_expert = lambda j, eid, nu: (layer, eid[j], 0, 0)
    gs = pltpu.PrefetchScalarGridSpec(
        num_scalar_prefetch=2, grid=(n_tiles,),
        in_specs=[pl.BlockSpec((MOE_TM, d), tile),
                  pl.BlockSpec((None, None, d, 2 * D_EXPERT), per_expert),
                  pl.BlockSpec((None, None, 1, 2 * D_EXPERT), per_expert),
                  pl.BlockSpec((None, None, D_EXPERT, d), per_expert),
                  pl.BlockSpec((None, None, 1, d), per_expert),
                  pl.BlockSpec((2 * D_EXPERT, D_EXPERT), lambda j, eid, nu: (0, 0))],
        out_specs=pl.BlockSpec((MOE_TM, d), lambda j, eid, nu: (j, 0)))
    return pl.pallas_call(
        _expert_kernel, grid_spec=gs, out_shape=jax.ShapeDtypeStruct((n_rows, d), F32),
        compiler_params=_cparams(("arbitrary",)), name="moe_experts",
    )(tile_eid, nused, xs, wgu, bgu, wd, bd, jnp.asarray(pick, BF16))


def _combine_ln_kernel(dest_cur_ref, dest_nxt_ref, x_ref, route_ref, g_ref, b_ref, ys_hbm,
                       of_ref, ob_ref, ybuf, sems, *, tm):
    i = pl.program_id(0)
    slot = i % 2

    per_row = LANE // TOP_K

    def gather(dest_ref, s):
        for q in range(tm // per_row):
            def issue(rr, c, q=q):
                for k in range(TOP_K):
                    src_row = dest_ref[q, TOP_K * rr + k]
                    _row_copy(ys_hbm, src_row, ybuf.at[s, k], q * per_row + rr,
                              sems.at[s]).start(priority=k % 2)
                return c
            lax.fori_loop(0, per_row, issue, 0)

    @pl.when(i == 0)
    def _():
        gather(dest_cur_ref, slot)

    @pl.when(i + 1 < pl.num_programs(0))
    def _():
        gather(dest_nxt_ref, 1 - slot)

    def drain(r, c):
        for k in range(TOP_K):
            _row_copy(ys_hbm, 0, ybuf.at[slot, k], 0, sems.at[slot]).wait()
        return c

    lax.fori_loop(0, tm, drain, 0)
    route = route_ref[...]
    z = DEEPNORM_ALPHA * x_ref[...]
    for k in range(TOP_K):
        z = z + route[:, ROUTE_W + k:ROUTE_W + k + 1] * ybuf[slot, k]
    mu = jnp.mean(z, axis=-1, keepdims=True)
    zc = z - mu
    var = jnp.mean(zc * zc, axis=-1, keepdims=True)
    out = zc * lax.rsqrt(var + 1e-5) * g_ref[...] + b_ref[...]
    of_ref[...] = out
    ob_ref[...] = out.astype(BF16)


def _combine_ln(xf, ys, dest_flat, route, g, b, tm=128):
    t, d = xf.shape
    tm = min(tm, t)
    n = t // tm
    row = pl.BlockSpec((tm, d), lambda i: (i, 0))
    lanes = pl.BlockSpec((tm, LANE), lambda i: (i, 0))
    vec = pl.BlockSpec((1, d), lambda i: (0, 0))
    kern = functools.partial(_combine_ln_kernel, tm=tm)
    idx_rows = TOP_K * tm // LANE
    dest3 = dest_flat.reshape(n, idx_rows, LANE)
    idx_blk = lambda f: pl.BlockSpec((None, idx_rows, LANE), f, memory_space=pltpu.SMEM)
    return pl.pallas_call(
        kern, grid=(n,),
        in_specs=[idx_blk(lambda i: (i, 0, 0)), idx_blk(lambda i: (jnp.minimum(i + 1, n - 1), 0, 0)),
                  row, lanes, vec, vec, pl.BlockSpec(memory_space=pl.ANY)],
        out_specs=[row, row],
        out_shape=[jax.ShapeDtypeStruct((t, d), F32), jax.ShapeDtypeStruct((t, d), BF16)],
        scratch_shapes=[pltpu.VMEM((2, TOP_K, tm, d), F32), pltpu.SemaphoreType.DMA((2,))],
        compiler_params=_cparams(("arbitrary",)), name="moe_combine_ln",
    )(dest3, dest3, xf, route, g.reshape(1, d), b.reshape(1, d), ys)


def _moe_ln(xf, rw, rb, wgu, bgu, wd, bd, g, b, layer):
    t, d = xf.shape
    route, cnt = _router(xf, rw, rb)
    cnt = cnt[0, :N_EXPERTS].astype(jnp.int32)
    tiles_e = jnp.maximum((cnt + MOE_TM - 1) // MOE_TM, 1)
    gend = jnp.cumsum(tiles_e) * MOE_TM
    gstart = gend - tiles_e * MOE_TM
    eid = route[:, ROUTE_EID:ROUTE_EID + TOP_K].astype(jnp.int32)
    rank = route[:, ROUTE_RANK:ROUTE_RANK + TOP_K].astype(jnp.int32)
    onehot = eid[..., None] == jnp.arange(N_EXPERTS, dtype=jnp.int32)
    dest = jnp.sum(jnp.where(onehot, gstart, 0), axis=-1) + rank
    dest_flat = dest.reshape(-1).astype(jnp.int32)
    n_tiles = (t * TOP_K) // MOE_TM + N_EXPERTS
    tile_row = jnp.arange(n_tiles, dtype=jnp.int32)[:, None] * MOE_TM
    tile_eid = jnp.minimum(jnp.sum((gend[None, :] <= tile_row).astype(jnp.int32), axis=1), N_EXPERTS - 1)
    nused = (gend[-1:] // MOE_TM).astype(jnp.int32)
    xs = _dispatch(xf, dest_flat, gend.astype(jnp.int32), n_tiles * MOE_TM)
    ys = _experts(xs, tile_eid, nused, wgu, bgu, wd, bd, layer)
    return _combine_ln(xf, ys, dest_flat, route, g, b)


def _compress(kv, pe, w1, b1, w2, b2, t, rope_tab):
    g = A_KV_GROUPS
    nch = t // CMP_STRIDE
    n_pad = nch
    chunks = kv.reshape(nch, CMP_STRIDE, g, HEAD_DIM)
    blocks = jnp.concatenate([chunks[:-1], chunks[1:]], axis=1)
    flat = blocks.transpose(2, 0, 1, 3).reshape(g, nch - 1, CMP_LEN * HEAD_DIM)
    flat = jnp.pad(flat, ((0, 0), (0, 1), (0, 0))).reshape(g * n_pad, CMP_LEN * HEAD_DIM)
    w1b = w1.astype(BF16)
    pe_rows = jnp.zeros((8, CMP_LEN * HEAD_DIM), F32).at[0].set(pe.reshape(-1)).astype(BF16)
    pe_term = _mm(pe_rows, w1b, out_dtype=F32, name="cmp_pe")[0]
    hid = _mm(flat, w1b, bias=pe_term + b1, act="gelu", name="cmp_mlp1")
    if rope_tab is None:
        return _mm(hid, w2.astype(BF16), bias=b2, tn=LANE, name="cmp_mlp2")
    return _mm(hid, w2.astype(BF16), bias=b2, tn=LANE, rope_tabs=rope_tab[None],
               rope_modes=jnp.ones((1,), jnp.int32), name="cmp_mlp2_rope")


def _even_w_kernel(a_ref, b_ref, o_ref, *, first_shifted, gate_tile):
    j = pl.program_id(1)
    a = a_ref[...]
    tn = a.shape[1]

    @pl.when(j < first_shifted)
    def _():
        o_ref[...] = a.astype(o_ref.dtype)

    @pl.when((j >= first_shifted) & (j < gate_tile))
    def _():
        b = b_ref[...]
        o_ref[...] = jnp.concatenate([a[:, A_GATE:], b[:, :A_GATE]], axis=1).astype(o_ref.dtype)

    @pl.when(j == gate_tile)
    def _():
        lane = lax.broadcasted_iota(jnp.int32, a.shape, 1)
        o_ref[...] = jnp.where(lane < A_GATE, a, 0.0).astype(o_ref.dtype)


def _even_w_layout(w_in, e, tr=512):
    d = w_in.shape[1]
    tr = min(tr, d)
    first_shifted = (A_Q + 6 * A_KV) // MM_TN
    gate_tile = EVEN_COLS // MM_TN - 1
    last_in = (w_in.shape[2] - 1) // MM_TN
    kern = functools.partial(_even_w_kernel, first_shifted=first_shifted, gate_tile=gate_tile)
    return pl.pallas_call(
        kern, grid=(d // tr, EVEN_COLS // MM_TN),
        in_specs=[pl.BlockSpec((None, tr, MM_TN),
                               lambda i, j: (e, i, jnp.where(j == gate_tile, first_shifted, j))),
                  pl.BlockSpec((None, tr, MM_TN), lambda i, j: (e, i, jnp.minimum(j + 1, last_in)))],
        out_specs=pl.BlockSpec((tr, MM_TN), lambda i, j: (i, j)),
        out_shape=jax.ShapeDtypeStruct((d, EVEN_COLS), BF16),
        compiler_params=_cparams(("parallel", "arbitrary")), name="even_w_layout",
    )(w_in, w_in)


def _even_mixer(xb, w_in, w_out, e, cmpk, cmpv, tabs):
    t, d = xb.shape
    w = _even_w_layout(w_in, e)
    tile_modes = np.zeros((EVEN_COLS // MM_TN,), np.int32)
    per = MM_TN // LANE
    for cb, nblk, mode in ((CB_QA, A_Q // LANE, 3), (CB_KS, A_KV // LANE, 1), (CB_KW, A_KV // LANE, 1),
                           (CB_QB, B_W // LANE, 2), (CB_KB, B_W // LANE, 1)):
        tile_modes[cb // per:(cb + nblk) // per] = mode
    h = _mm(xb, w, rope_tabs=tabs["qk"], rope_modes=jnp.asarray(tile_modes), name="even_in_proj")

    n_pad = t // CMP_STRIDE
    kc = _compress(h[:, CB_KC * LANE:CB_VC * LANE], *cmpk, t, jnp.tile(tabs["cmp"], (A_KV_GROUPS, 1)))
    vc = _compress(h[:, CB_VC * LANE:CB_KS * LANE], *cmpv, t, None)
    o_c, sel = _cmp_attn(h, kc, vc, t)
    o_s = _sel_attn(h, sel, t)

    tqw = min(WIN_LEN, t)
    (o_w,) = _swa(h, h, h, n_r=A_KV_GROUPS, n_tiles=t // tqw, tq=tqw, nh=A_HPG, shared_kv=True,
                  max_dist=WIN_LEN - 1, q_map=lambda r: r, k_map=lambda r: CB_KW + r,
                  v_map=lambda r: CB_VW + r, out_cols=A_Q, o_map=lambda r: r, with_lse=False,
                  log2_scores=True,
                  name="nsa_window")

    ob_parts, lse_parts = [], []
    for gi, (window, dil) in enumerate(B_DILATIONS):
        if dil == 1:
            tqd = min(max(window, LANE), t)
            o, lse = _swa(h, h, h, n_r=1, n_tiles=t // tqd, tq=tqd, nh=B_HPG, shared_kv=False,
                          max_dist=window, q_map=lambda r, b=CB_QB // B_HPG + gi: b,
                          k_map=lambda r, b=CB_KB // B_HPG + gi: b,
                          v_map=lambda r, b=CB_VB // B_HPG + gi: b,
                          out_cols=B_HPG * LANE, o_map=lambda r: r, with_lse=True, log2_scores=False,
                          name="dilated_1")
        else:
            o, lse = _band_attn(h, t, q_cb=CB_QB + gi * B_HPG, k_cb=CB_KB + gi * B_HPG,
                                v_cb=CB_VB + gi * B_HPG, nh=B_HPG, window=window, dil=dil)
        ob_parts.append(o)
        lse_parts.append(lse)
    o_b = jnp.concatenate(ob_parts, axis=1)
    lse_b = jnp.concatenate(lse_parts, axis=1)
    mix_in = _mixprep(o_c, o_s, o_w, h, o_b, lse_b, t)
    return _mm(mix_in, w_out, out_dtype=F32, b_layer=e, name="even_out_proj")


def _odd_mixer(xb, w_in, w_out, o_idx, lq1, lk1, lq2, lk2, sub_g, layer, tabs):
    t, d = xb.shape
    per = MM_TN // LANE
    tile_modes = np.zeros(((2 * C_QK + C_V) // MM_TN,), np.int32)
    tile_modes[:C_QK // MM_TN] = 3
    tile_modes[C_QK // MM_TN:2 * C_QK // MM_TN] = 1
    h = _mm(xb, w_in, rope_tabs=tabs["qk"], rope_modes=jnp.asarray(tile_modes), b_layer=o_idx,
            name="odd_in_proj")
    o = _diff_attn(h, lq1, lk1, lq2, lk2, sub_g, layer, t)
    return _mm(o, w_out, out_dtype=F32, b_layer=o_idx, name="odd_out_proj")


def kernel(x, even_w_in, even_w_out, cmpk_pe, cmpk_w1, cmpk_b1, cmpk_w2, cmpk_b2, cmpv_pe, cmpv_w1, cmpv_b1, cmpv_w2, cmpv_b2, odd_w_in, odd_w_out, lam_q1, lam_k1, lam_q2, lam_k2, subln_g, ln_mix_g, ln_mix_b, ln_ffn_g, ln_ffn_b, router_w, router_b, exp_w_gu, exp_b_gu, exp_w_down, exp_b_down):
    bsz, t, d = x.shape
    assert bsz == 1
    xf = x.reshape(t, d)
    xb = xf.astype(BF16)
    pos = jnp.arange(t, dtype=jnp.int32)
    cmp_end = jnp.arange(t // CMP_STRIDE, dtype=jnp.int32) * CMP_STRIDE + (CMP_LEN - 1)
    tabs = {
        "qk": jnp.stack([_rope_tables(pos, 1.0), _rope_tables(pos, HEAD_DIM ** -0.5),
                         _rope_tables(pos, HEAD_DIM ** -0.5 * math.log2(math.e))]),
        "cmp": _rope_tables(cmp_end, 1.0),
    }
    even_w_out_b = even_w_out.astype(BF16)
    odd_w_in_b = odd_w_in.astype(BF16)
    odd_w_out_b = odd_w_out.astype(BF16)
    wgu_b = exp_w_gu.astype(BF16)
    wd_b = exp_w_down.astype(BF16)
    bgu = exp_b_gu[:, :, None, :].astype(F32)
    bd = exp_b_down[:, :, None, :].astype(F32)
    for layer in range(DEPTH):
        if layer % 2 == 0:
            e = layer // 2
            mix = _even_mixer(xb, even_w_in, even_w_out_b, e,
                              (cmpk_pe[e], cmpk_w1[e], cmpk_b1[e], cmpk_w2[e], cmpk_b2[e]),
                              (cmpv_pe[e], cmpv_w1[e], cmpv_b1[e], cmpv_w2[e], cmpv_b2[e]), tabs)
        else:
            o = layer // 2
            mix = _odd_mixer(xb, odd_w_in_b, odd_w_out_b, o, lam_q1[o], lam_k1[o], lam_q2[o],
                             lam_k2[o], subln_g[o], layer, tabs)
        xf, xb = _ln_res(xf, mix, ln_mix_g[layer], ln_mix_b[layer])
        xf, xb = _moe_ln(xf, router_w[layer], router_b[layer], wgu_b, bgu, wd_b, bd,
                         ln_ffn_g[layer], ln_ffn_b[layer], layer)
    return xf.reshape(bsz, t, d)
```
